```python
import jax, jax.numpy as jnp
from jax import lax
import numpy as np

D_MODEL = 1024
BATCH = 4
SEQ = 8192
DEPTH = 1
DEC_BATCH = 32
DEC_SEQ = 4
PAST_LEN = 16384
PAGE_SIZE = 128

N_META = 16
GLA_HEADS = 4
GLA_DK = D_MODEL // 16
GLA_DV = D_MODEL // 8
GLA_RANK = 16
GLA_TAU = 16.0
GLA_CHUNK = 64
GLA_QK_W = GLA_HEADS * GLA_DK
GLA_V_W = GLA_HEADS * GLA_DV
SB_HEADS = D_MODEL // 128
SB_DIM = 64
SB_W = SB_HEADS * SB_DIM
SB_BLOCK = 128
SB_BIAS_INIT = -6.0
FRONT_PAD = SB_BLOCK - N_META
D_IN = 2 * GLA_QK_W + 2 * GLA_V_W + GLA_RANK + 3 * SB_W
N_GROUPS = 4
EXPERTS_PER_GROUP = 8
N_EXPERTS = N_GROUPS * EXPERTS_PER_GROUP
TOP_K = 2
D_EXPERT = D_MODEL // 2
MOE_BLOCK = 128
EPS = 1e-6

kernel_name = "hymba_gla_stickbreaking_hmoe_step"


def rms_norm(x, gain):
    xf = x.astype(jnp.float32)
    y = xf * lax.rsqrt(jnp.mean(xf * xf, axis=-1, keepdims=True) + EPS)
    return (y * gain.astype(jnp.float32)).astype(x.dtype)


def mixer_inputs(h, norm_g, w_in, w_alpha, b_alpha, q_gain, k_gain):
    B, T, _ = h.shape
    p = rms_norm(h, norm_g) @ w_in
    sizes = [GLA_QK_W, GLA_QK_W, GLA_V_W, GLA_V_W, GLA_RANK, SB_W, SB_W, SB_W]
    offs = [int(o) for o in np.cumsum(sizes)[:-1]]
    gq, gk, gv, gr, ga, sq, sk, sv = jnp.split(p, offs, axis=-1)
    q_g = gq.reshape(B, T, GLA_HEADS, GLA_DK) * (GLA_DK ** -0.5)
    k_g = gk.reshape(B, T, GLA_HEADS, GLA_DK)
    v_g = gv.reshape(B, T, GLA_HEADS, GLA_DV)
    log_a = (jax.nn.log_sigmoid((ga @ w_alpha + b_alpha).astype(jnp.float32)) / GLA_TAU).reshape(B, T, GLA_HEADS, GLA_DK)
    q_s = rms_norm(sq.reshape(B, T, SB_HEADS, SB_DIM), q_gain)
    k_s = rms_norm(sk.reshape(B, T, SB_HEADS, SB_DIM), k_gain)
    v_s = sv.reshape(B, T, SB_HEADS, SB_DIM)
    return q_g, k_g, v_g, log_a, gr, q_s, k_s, v_s


def gla_chunk(S, q, k, v, log_a):
    S = S.astype(jnp.float32)
    q, k, v = q.astype(jnp.float32), k.astype(jnp.float32), v.astype(jnp.float32)
    b = jnp.cumsum(log_a.astype(jnp.float32), axis=2)
    C = q.shape[2]
    causal = jnp.tril(jnp.ones((C, C), bool))
    diff = b[:, :, :, None, :] - b[:, :, None, :, :]
    decay = jnp.exp(jnp.where(causal[None, None, :, :, None], diff, -jnp.inf))
    scores = jnp.einsum('bhtd,bhsd,bhtsd->bhts', q, k, decay)
    o = jnp.einsum('bhts,bhsv->bhtv', scores, v) + jnp.einsum('bhtd,bhdv->bhtv', q * jnp.exp(b), S)
    b_last = b[:, :, -1:, :]
    S_new = jnp.exp(b_last[:, :, 0, :, None]) * S + jnp.einsum('bhsd,bhsv->bhdv', k * jnp.exp(b_last - b), v)
    return S_new, o


def gla_prompt(q, k, v, log_a):
    B, Tp, H, _ = q.shape
    n_chunks = Tp // GLA_CHUNK

    def to_chunks(t):
        return t.reshape(B, n_chunks, GLA_CHUNK, H, t.shape[-1]).transpose(1, 0, 3, 2, 4)

    S0 = jnp.zeros((B, H, GLA_DK, GLA_DV), jnp.float32)

    def step(S, xs):
        return gla_chunk(S, *xs)

    S, o = lax.scan(step, S0, (to_chunks(q), to_chunks(k), to_chunks(v), to_chunks(log_a)))
    return S, o.transpose(1, 0, 3, 2, 4).reshape(B, Tp, H, GLA_DV)


def stick_breaking(q, k, v, mask, logit_bias):
    z = jnp.einsum('bqhd,bkhd->bhqk', q, k).astype(jnp.float32) * (SB_DIM ** -0.5) \
        + logit_bias.astype(jnp.float32)[None, :, None, None]
    log_beta = jax.nn.log_sigmoid(z)
    log_1mb = jnp.where(mask, jax.nn.log_sigmoid(-z), 0.0)
    between = lax.cumsum(log_1mb, axis=z.ndim - 1, reverse=True) - log_1mb
    a = jnp.where(mask, jnp.exp(log_beta + between), 0.0)
    return jnp.einsum('bhqk,bkhd->bqhd', a, v.astype(jnp.float32))


def sb_prompt(q, k, v, logit_bias):
    B, Tp, H, D = q.shape
    n_blk = Tp // SB_BLOCK
    key_pos = jnp.arange(Tp)

    def block(i):
        q0 = i * SB_BLOCK
        qb = lax.dynamic_slice_in_dim(q, q0, SB_BLOCK, axis=1)
        q_pos = q0 + jnp.arange(SB_BLOCK)
        mask = (key_pos[None, :] < q_pos[:, None]) & (key_pos[None, :] >= FRONT_PAD)
        return stick_breaking(qb, k, v, mask, logit_bias)

    o = lax.map(block, jnp.arange(n_blk))
    return o.transpose(1, 0, 2, 3, 4).reshape(B, Tp, H, D)


def sb_sample(q, k_new, v_new, cache_k, cache_v, page_table, logit_bias):
    B, S, H, D = q.shape
    past = page_table.shape[1] * cache_k.shape[1]
    k_past = cache_k[page_table].reshape(B, past, H, D)
    v_past = cache_v[page_table].reshape(B, past, H, D)
    k_all = jnp.concatenate([k_past, k_new.astype(k_past.dtype)], axis=1)
    v_all = jnp.concatenate([v_past, v_new.astype(v_past.dtype)], axis=1)
    key_pos = jnp.arange(past + S)
    q_pos = past + jnp.arange(S)
    mask = key_pos[None, :] < q_pos[:, None]
    return stick_breaking(q, k_all, v_all, mask, logit_bias)


def mixer_output(o_g, gr, o_s, gla_out_gain, sb_out_gain, w_out):
    B, T = gr.shape[:2]
    og = rms_norm(o_g, gla_out_gain).reshape(B, T, GLA_V_W) * jax.nn.silu(gr)
    osb = rms_norm(o_s, sb_out_gain).reshape(B, T, SB_W)
    return jnp.concatenate([og, osb], axis=-1) @ w_out


def hier_moe(x2, router_group, router_group_b, router_expert, router_expert_b, w_gate, w_up, w_down):
    N, D = x2.shape
    xf = x2.astype(jnp.float32)
    p_group = jax.nn.softmax(xf @ router_group.astype(jnp.float32) + router_group_b, axis=-1)
    g_idx = jnp.argmax(p_group, axis=-1)
    g_val = jnp.take_along_axis(p_group, g_idx[:, None], axis=-1)
    logits_all = jnp.einsum('nd,gde->nge', xf, router_expert.astype(jnp.float32)) + router_expert_b
    logits_e = jnp.take_along_axis(logits_all, g_idx[:, None, None], axis=1)[:, 0]
    top_p, top_i = lax.top_k(jax.nn.softmax(logits_e, axis=-1), TOP_K)
    gate = g_val * top_p / jnp.sum(top_p, axis=-1, keepdims=True)
    expert_idx = g_idx[:, None] * EXPERTS_PER_GROUP + top_i
    M = N * TOP_K
    flat_e = expert_idx.reshape(M).astype(jnp.int32)
    flat_tok = jnp.repeat(jnp.arange(N, dtype=jnp.int32), TOP_K)
    flat_w = gate.reshape(M)
    order = jnp.argsort(flat_e)
    sorted_e = flat_e[order]
    counts = jnp.zeros((N_EXPERTS,), jnp.int32).at[flat_e].add(1)
    starts = jnp.cumsum(counts) - counts
    padded = (counts + MOE_BLOCK - 1) // MOE_BLOCK * MOE_BLOCK
    padded_ends = jnp.cumsum(padded)
    padded_starts = padded_ends - padded
    dest = padded_starts[sorted_e] + jnp.arange(M, dtype=jnp.int32) - starts[sorted_e]
    n_blocks = (M + MOE_BLOCK - 1) // MOE_BLOCK + N_EXPERTS
    m_pad = n_blocks * MOE_BLOCK
    row_tok = jnp.full((m_pad,), N, jnp.int32).at[dest].set(flat_tok[order])
    row_w = jnp.zeros((m_pad,), jnp.float32).at[dest].set(flat_w[order])
    block_e = jnp.minimum(jnp.searchsorted(padded_ends, jnp.arange(n_blocks, dtype=jnp.int32) * MOE_BLOCK, side='right'), N_EXPERTS - 1)
    x_rows = jnp.concatenate([x2, jnp.zeros((1, D), x2.dtype)], axis=0)[row_tok].reshape(n_blocks, MOE_BLOCK, D)

    def expert_block(args):
        xb, e = args
        hdn = jax.nn.silu(xb @ w_gate[e]) * (xb @ w_up[e])
        return hdn @ w_down[e]

    y_rows = lax.map(expert_block, (x_rows, block_e)).reshape(m_pad, D)
    y = jnp.zeros((N + 1, D), jnp.float32).at[row_tok].add(y_rows.astype(jnp.float32) * row_w[:, None])[:N]
    return y.astype(x2.dtype)


def setup_inputs(seed: int = 0) -> dict:
    key = jax.random.key(seed)
    ks = jax.random.split(key, 26)
    n_pages = PAST_LEN // PAGE_SIZE
    n_phys = (DEC_BATCH * n_pages * 5) // 4

    def nrm(k, shape, scale):
        return jax.random.normal(k, shape, jnp.float32) * scale

    def gain(k, shape):
        return 1.0 + 0.02 * jax.random.normal(k, shape, jnp.float32)

    page_table = jax.random.permutation(ks[5], n_phys)[:DEC_BATCH * n_pages].reshape(DEC_BATCH, n_pages).astype(jnp.int32)
    return {
        "x_prompt": nrm(ks[0], (BATCH, SEQ, D_MODEL), 1.0),
        "x_sample": nrm(ks[1], (DEC_BATCH, DEC_SEQ, D_MODEL), 1.0),
        "cache_sb_k": nrm(ks[2], (DEPTH, n_phys, PAGE_SIZE, SB_HEADS, SB_DIM), 1.0),
        "cache_sb_v": nrm(ks[3], (DEPTH, n_phys, PAGE_SIZE, SB_HEADS, SB_DIM), 1.0),
        "state_gla": nrm(ks[4], (DEPTH, DEC_BATCH, GLA_HEADS, GLA_DK, GLA_DV), 0.1),
        "page_table": page_table,
        "meta_tokens": nrm(ks[6], (N_META, D_MODEL), 1.0),
        "norm_mix_gain": gain(ks[7], (DEPTH, D_MODEL)),
        "w_in": nrm(ks[8], (DEPTH, D_MODEL, D_IN), D_MODEL ** -0.5),
        "gla_w_alpha": nrm(ks[9], (DEPTH, GLA_RANK, GLA_QK_W), GLA_RANK ** -0.5),
        "gla_b_alpha": nrm(ks[10], (DEPTH, GLA_QK_W), 0.1),
        "gla_out_gain": gain(ks[11], (DEPTH, GLA_DV)),
        "sb_q_gain": gain(ks[12], (DEPTH, SB_DIM)),
        "sb_k_gain": gain(ks[13], (DEPTH, SB_DIM)),
        "sb_logit_bias": SB_BIAS_INIT + nrm(ks[24], (DEPTH, SB_HEADS), 0.5),
        "sb_out_gain": gain(ks[14], (DEPTH, SB_DIM)),
        "w_out": nrm(ks[15], (DEPTH, D_MODEL, D_MODEL), D_MODEL ** -0.5),
        "norm_ffn_gain": gain(ks[16], (DEPTH, D_MODEL)),
        "router_group": nrm(ks[17], (DEPTH, D_MODEL, N_GROUPS), D_MODEL ** -0.5),
        "router_group_b": nrm(ks[18], (DEPTH, N_GROUPS), 0.01),
        "router_expert": nrm(ks[19], (DEPTH, N_GROUPS, D_MODEL, EXPERTS_PER_GROUP), D_MODEL ** -0.5),
        "router_expert_b": nrm(ks[20], (DEPTH, N_GROUPS, EXPERTS_PER_GROUP), 0.01),
        "w_gate": nrm(ks[21], (DEPTH, N_EXPERTS, D_MODEL, D_EXPERT), D_MODEL ** -0.5),
        "w_up": nrm(ks[22], (DEPTH, N_EXPERTS, D_MODEL, D_EXPERT), D_MODEL ** -0.5),
        "w_down": nrm(ks[23], (DEPTH, N_EXPERTS, D_EXPERT, D_MODEL), D_EXPERT ** -0.5),
    }


def reference(x_prompt, x_sample, cache_sb_k, cache_sb_v, state_gla, page_table, meta_tokens,
              norm_mix_gain, w_in, gla_w_alpha, gla_b_alpha, gla_out_gain, sb_q_gain, sb_k_gain,
              sb_logit_bias, sb_out_gain, w_out, norm_ffn_gain, router_group, router_group_b,
              router_expert, router_expert_b, w_gate, w_up, w_down):
    B, _, D = x_prompt.shape
    hp = jnp.concatenate([jnp.broadcast_to(meta_tokens[None].astype(x_prompt.dtype), (B, N_META, D)), x_prompt], axis=1)
    hs = x_sample

    def pad(t):
        return jnp.pad(t, ((0, 0), (FRONT_PAD, 0), (0, 0), (0, 0)))

    k_p_list, v_p_list, s_p_list, k_s_list, v_s_list, s_s_list = [], [], [], [], [], []
    for l in range(DEPTH):
        q_g, k_g, v_g, log_a, gr, q_s, k_s, v_s = mixer_inputs(hp, norm_mix_gain[l], w_in[l], gla_w_alpha[l], gla_b_alpha[l], sb_q_gain[l], sb_k_gain[l])
        S_p, o_g = gla_prompt(pad(q_g), pad(k_g), pad(v_g), pad(log_a))
        o_s = sb_prompt(pad(q_s), pad(k_s), pad(v_s), sb_logit_bias[l])
        hp = hp + mixer_output(o_g[:, FRONT_PAD:], gr, o_s[:, FRONT_PAD:], gla_out_gain[l], sb_out_gain[l], w_out[l])
        Tp = hp.shape[1]
        hp = hp + hier_moe(rms_norm(hp, norm_ffn_gain[l]).reshape(B * Tp, D), router_group[l], router_group_b[l], router_expert[l], router_expert_b[l], w_gate[l], w_up[l], w_down[l]).reshape(B, Tp, D)
        k_p_list.append(k_s)
        v_p_list.append(v_s)
        s_p_list.append(S_p.astype(x_prompt.dtype))
        Bs, Ts = hs.shape[:2]
        q_g, k_g, v_g, log_a, gr, q_s, k_s, v_s = mixer_inputs(hs, norm_mix_gain[l], w_in[l], gla_w_alpha[l], gla_b_alpha[l], sb_q_gain[l], sb_k_gain[l])
        S_s, o_g = gla_chunk(state_gla[l], q_g.transpose(0, 2, 1, 3), k_g.transpose(0, 2, 1, 3), v_g.transpose(0, 2, 1, 3), log_a.transpose(0, 2, 1, 3))
        o_s = sb_sample(q_s, k_s, v_s, cache_sb_k[l], cache_sb_v[l], page_table, sb_logit_bias[l])
        hs = hs + mixer_output(o_g.transpose(0, 2, 1, 3), gr, o_s, gla_out_gain[l], sb_out_gain[l], w_out[l])
        hs = hs + hier_moe(rms_norm(hs, norm_ffn_gain[l]).reshape(Bs * Ts, D), router_group[l], router_group_b[l], router_expert[l], router_expert_b[l], w_gate[l], w_up[l], w_down[l]).reshape(Bs, Ts, D)
        k_s_list.append(k_s)
        v_s_list.append(v_s)
        s_s_list.append(S_s.astype(x_sample.dtype))

    y_prompt = hp[:, N_META:].astype(x_prompt.dtype)
    y_sample = hs.astype(x_sample.dtype)
    new_k_prompt = jnp.stack(k_p_list)
    new_v_prompt = jnp.stack(v_p_list)
    new_gla_prompt = jnp.stack(s_p_list)
    new_k_sample = jnp.stack(k_s_list)
    new_v_sample = jnp.stack(v_s_list)
    new_gla_sample = jnp.stack(s_s_list)
    return (y_prompt, y_sample, new_k_prompt, new_v_prompt, new_gla_prompt, new_k_sample, new_v_sample, new_gla_sample)
```

```python
import functools

import jax
import jax.numpy as jnp
import numpy as np
from jax import lax
from jax.experimental import pallas as pl
from jax.experimental.pallas import tpu as pltpu

F32 = jnp.float32
BF16 = jnp.bfloat16

N_META = 16
GLA_HEADS = 4
GLA_DK = 64
GLA_DV = 128
GLA_RANK = 16
GLA_TAU = 16.0
GLA_QK_W = GLA_HEADS * GLA_DK
GLA_V_W = GLA_HEADS * GLA_DV
SB_HEADS = 8
SB_DIM = 64
SB_W = SB_HEADS * SB_DIM
N_GROUPS = 4
EXPERTS_PER_GROUP = 8
N_EXPERTS = N_GROUPS * EXPERTS_PER_GROUP
EPS = 1e-6

LANES = 128
SUBLANES = 8
VMEM_LIMIT_BYTES = 56 * 1024 * 1024

ROW_TILE = 512
GLA_CHUNK = 128
SB_TQ = 768
SB_TK = 256
MOE_TILE = 1024
MOE_BLOCK = 256

_C_GQ, _C_GK, _C_GV, _C_GR, _C_SQ, _C_SK, _C_SV, _C_GA = 0, 256, 512, 1024, 1536, 2048, 2560, 3072
_W_IN_COLS = 3200


def _cparams(sem):
    return pltpu.CompilerParams(dimension_semantics=sem, vmem_limit_bytes=VMEM_LIMIT_BYTES)


def _dot(a, b):
    return jnp.dot(a, b, preferred_element_type=F32)


def _dot_nt(a, b):
    return lax.dot_general(a, b, (((1,), (1,)), ((), ())), preferred_element_type=F32)


def _split_bf16(x):
    hi = x.astype(BF16)
    lo = (x - hi.astype(F32)).astype(BF16)
    return hi, lo


def _lane_iota(shape):
    return lax.broadcasted_iota(jnp.int32, shape, len(shape) - 1)


def _half_lane_rms(x, gain):
    outs = []
    for g in range(x.shape[1] // LANES):
        xg = x[:, g * LANES:(g + 1) * LANES]
        x2 = xg * xg
        low = _lane_iota(xg.shape) < SB_DIM
        s_all = jnp.sum(x2, axis=-1, keepdims=True)
        s_lo = jnp.sum(jnp.where(low, x2, 0.0), axis=-1, keepdims=True)
        ms = jnp.where(low, s_lo, s_all - s_lo) * (1.0 / SB_DIM)
        outs.append(xg * lax.rsqrt(ms + EPS))
    return jnp.concatenate(outs, axis=1) * gain


def _in_proj_kernel(x_ref, g_ref, w_ref, wa_ref, ba_ref, qgain_ref, kgain_ref,
                    gq_ref, gk_ref, gv_ref, gr_ref, la_ref, qs_ref, ks_ref, vs_ref, ksb_ref, vsb_ref):
    x = x_ref[...]
    ms = jnp.mean(x * x, axis=-1, keepdims=True)
    xn = ((x * lax.rsqrt(ms + EPS)) * g_ref[...]).astype(BF16)

    def proj(c0, width):
        return _dot(xn, w_ref[:, c0:c0 + width])

    gq_ref[...] = proj(_C_GQ, GLA_QK_W) * (GLA_DK ** -0.5)
    gk_ref[...] = proj(_C_GK, GLA_QK_W)
    gv_ref[...] = proj(_C_GV, GLA_V_W)
    gr_ref[...] = proj(_C_GR, GLA_V_W)

    ga_hi, ga_lo = _split_bf16(proj(_C_GA, LANES))
    wa_hi, wa_lo = _split_bf16(wa_ref[...])
    u = _dot(ga_hi, wa_hi) + _dot(ga_lo, wa_hi) + _dot(ga_hi, wa_lo) + ba_ref[...]
    la_ref[...] = (jnp.minimum(u, 0.0) - jnp.log(1.0 + jnp.exp(-jnp.abs(u)))) * (1.0 / GLA_TAU)

    q_s = _half_lane_rms(proj(_C_SQ, SB_W), qgain_ref[...])
    qs_ref[...] = (q_s * (SB_DIM ** -0.5)).astype(BF16)
    k_s = _half_lane_rms(proj(_C_SK, SB_W), kgain_ref[...])
    ks_ref[...] = k_s
    ksb_ref[...] = k_s.astype(BF16)
    v_s = proj(_C_SV, SB_W)
    vs_ref[...] = v_s
    vsb_ref[...] = v_s.astype(BF16)


def _in_proj(h2, norm_g, w_in_r, wa_pad, ba, qgain, kgain):
    n, d = h2.shape
    tm = min(ROW_TILE, n)
    row = lambda w: pl.BlockSpec((tm, w), lambda i: (i, 0))
    full = lambda a: pl.BlockSpec(a.shape, lambda i: (0,) * a.ndim)
    outs = [(GLA_QK_W, F32), (GLA_QK_W, F32), (GLA_V_W, F32), (GLA_V_W, F32), (GLA_QK_W, F32),
            (SB_W, BF16), (SB_W, F32), (SB_W, F32), (SB_W, BF16), (SB_W, BF16)]
    return pl.pallas_call(
        _in_proj_kernel,
        grid=(n // tm,),
        in_specs=[row(d), full(norm_g), full(w_in_r), full(wa_pad), full(ba), full(qgain), full(kgain)],
        out_specs=[row(w) for w, _ in outs],
        out_shape=[jax.ShapeDtypeStruct((n, w), dt) for w, dt in outs],
        compiler_params=_cparams(("parallel",)),
        name="in_proj",
    )(h2, norm_g, w_in_r, wa_pad, ba, qgain, kgain)


def _gla_levels(c):
    levels = []
    l = c // 2
    while l >= 1:
        levels.append(l)
        l //= 2
    return levels


def _gla_constants(c):
    t = np.arange(c)
    tri = (t[None, :] <= t[:, None]).astype(np.float32)
    mats, masks = [tri], []
    for l in _gla_levels(c):
        mid = (t // (2 * l)) * (2 * l) + l
        mats.append((t[None, :] <= (mid[:, None] - 1)).astype(np.float32))
        same = (t[:, None] // (2 * l)) == (t[None, :] // (2 * l))
        masks.append((same & ((t[:, None] % (2 * l)) >= l) & ((t[None, :] % (2 * l)) < l)).astype(np.float32))
    masks.append(np.eye(c, dtype=np.float32))
    return np.concatenate(mats, axis=0), np.stack(masks)


def _gla_kernel(q_ref, k_ref, v_ref, la_ref, s0_ref, gmat_ref, mask_ref, o_ref, s_out_ref, st_ref,
                *, chunk, front_pad):
    c = pl.program_id(1)
    levels = _gla_levels(chunk)
    w = GLA_QK_W

    @pl.when(c == 0)
    def _():
        st_ref[...] = s0_ref[0].T

    q = q_ref[0]
    k = k_ref[0]
    la = la_ref[0]
    if front_pad:
        row = lax.broadcasted_iota(jnp.int32, la.shape, 0) + c * chunk
        la = jnp.where(row < front_pad, 0.0, la)
    la_hi, la_lo = _split_bf16(la)
    p = _dot(gmat_ref[...], jnp.concatenate([la_hi, la_lo], axis=1))
    p = p[:, :w] + p[:, w:]
    b = p[:chunk]
    lane = _lane_iota((1, w))
    head_masks = [(lane >= h * GLA_DK) & (lane < (h + 1) * GLA_DK) for h in range(GLA_HEADS)]

    scores = [jnp.zeros((chunk, chunk), F32) for _ in range(GLA_HEADS)]
    for i, _l in enumerate(levels):
        r = p[(i + 1) * chunk:(i + 2) * chunk]
        qt = q * jnp.exp(jnp.minimum(b - r, 0.0))
        kt = (k * jnp.exp(jnp.minimum(r - b, 0.0))).astype(BF16)
        for h in range(GLA_HEADS):
            qh = jnp.where(head_masks[h], qt, 0.0).astype(BF16)
            scores[h] = scores[h] + mask_ref[i] * _dot_nt(qh, kt)
    kb = k.astype(BF16)
    for h in range(GLA_HEADS):
        qh = jnp.where(head_masks[h], q, 0.0).astype(BF16)
        scores[h] = scores[h] + mask_ref[len(levels)] * _dot_nt(qh, kb)

    st = st_ref[...]
    st_b = st.astype(BF16)
    b_last = b[chunk - 1:chunk]
    q_in = q * jnp.exp(b)
    k_out = k * jnp.exp(b_last - b)
    upd = jnp.zeros_like(st)
    for h in range(GLA_HEADS):
        vh = v_ref[0, :, h * GLA_DV:(h + 1) * GLA_DV]
        vhb = vh.astype(BF16)
        qh = jnp.where(head_masks[h], q_in, 0.0).astype(BF16)
        o_ref[0, :, h * GLA_DV:(h + 1) * GLA_DV] = _dot(scores[h].astype(BF16), vhb) + _dot_nt(qh, st_b)
        kh = jnp.where(head_masks[h], k_out, 0.0).astype(BF16)
        upd = upd + _dot(vh.T.astype(BF16), kh)
    st_new = st * jnp.exp(b_last) + upd
    st_ref[...] = st_new

    @pl.when(c == pl.num_programs(1) - 1)
    def _():
        s_out_ref[0] = st_new.T


def _gla(q, k, v, la, s0, front_pad):
    bsz, t, _ = q.shape
    chunk = GLA_CHUNK
    gmat, masks = _gla_constants(chunk)
    gmat = jnp.asarray(gmat, BF16)
    masks = jnp.asarray(masks, F32)
    tok = lambda w: pl.BlockSpec((1, chunk, w), lambda b, c: (b, c, 0))
    per_b = pl.BlockSpec((1, GLA_QK_W, GLA_DV), lambda b, c: (b, 0, 0))
    return pl.pallas_call(
        functools.partial(_gla_kernel, chunk=chunk, front_pad=front_pad),
        grid=(bsz, t // chunk),
        in_specs=[tok(GLA_QK_W), tok(GLA_QK_W), tok(GLA_V_W), tok(GLA_QK_W), per_b,
                  pl.BlockSpec(gmat.shape, lambda b, c: (0, 0)),
                  pl.BlockSpec(masks.shape, lambda b, c: (0, 0, 0))],
        out_specs=[tok(GLA_V_W), per_b],
        out_shape=[jax.ShapeDtypeStruct((bsz, t, GLA_V_W), F32),
                   jax.ShapeDtypeStruct((bsz, GLA_QK_W, GLA_DV), F32)],
        scratch_shapes=[pltpu.VMEM((GLA_DV, GLA_QK_W), F32)],
        compiler_params=_cparams(("parallel", "arbitrary")),
        name="gla",
    )(q, k, v, la, s0, gmat, masks)


def _softplus(z):
    return jnp.maximum(z, 0.0) + jnp.log(1.0 + jnp.exp(-jnp.abs(z)))


def _sb_tile(qh, kb, vh, ntri, bias, carry, mask):
    z = _dot_nt(qh, kb) + bias
    sp = _softplus(z)
    if mask is not None:
        sp = jnp.where(mask, sp, 0.0)
    between = _dot(sp.astype(BF16), ntri)
    reps = z.shape[1] // LANES
    t = (z - sp) + between + jnp.concatenate([carry] * reps, axis=1)
    a = jnp.exp(t)
    if mask is not None:
        a = jnp.where(mask, a, 0.0)
    return _dot(a.astype(BF16), vh), jnp.sum(sp, axis=-1, keepdims=True)


def _sb_prompt_kernel(bias_ref, q_ref, k_ref, v_ref, ntri_ref, o_ref, acc_ref, c0_ref, c1_ref, *, tq, tk):
    hp = pl.program_id(1)
    i = pl.program_id(2)
    ndiag = tq // tk
    bias0 = bias_ref[2 * hp]
    bias1 = bias_ref[2 * hp + 1]
    q = q_ref[0]
    low_q = _lane_iota(q.shape) < SB_DIM
    zero = jnp.zeros_like(q)
    q0 = jnp.where(low_q, q, zero)
    q1 = jnp.where(low_q, zero, q)
    ntri = ntri_ref[...]
    acc_ref[...] = jnp.zeros_like(acc_ref)
    c0_ref[...] = jnp.zeros_like(c0_ref)
    c1_ref[...] = jnp.zeros_like(c1_ref)
    low_k = _lane_iota((tk, LANES)) < SB_DIM

    def step(kstart, r0, mask):
        kb = k_ref[0, pl.ds(kstart, tk), :]
        vb = v_ref[0, pl.ds(kstart, tk), :]
        zk = jnp.zeros_like(vb)
        v0 = jnp.where(low_k, vb, zk)
        v1 = jnp.where(low_k, zk, vb)
        rows = slice(r0, tq)
        o0, s0 = _sb_tile(q0[rows], kb, v0, ntri, bias0, c0_ref[rows, :], mask)
        o1, s1 = _sb_tile(q1[rows], kb, v1, ntri, bias1, c1_ref[rows, :], mask)
        acc_ref[rows, :] += o0 + o1
        c0_ref[rows, :] -= jnp.broadcast_to(s0, (tq - r0, LANES))
        c1_ref[rows, :] -= jnp.broadcast_to(s1, (tq - r0, LANES))

    for d in reversed(range(ndiag)):
        nr = tq - d * tk
        mask = lax.broadcasted_iota(jnp.int32, (nr, tk), 0) > lax.broadcasted_iota(jnp.int32, (nr, tk), 1)
        step(pl.multiple_of(i * tq + d * tk, tk), d * tk, mask)

    nfull = i * ndiag

    def body(it, carry):
        step(pl.multiple_of((nfull - 1 - it) * tk, tk), 0, None)
        return carry

    lax.fori_loop(0, nfull, body, 0)
    o_ref[0] = acc_ref[...]


def _sb_prompt(q, k, v, bias):
    bsz, t, _ = q.shape
    tq, tk = SB_TQ, SB_TK
    t_idx = np.arange(tk)
    ntri = jnp.asarray(-(t_idx[:, None] > t_idx[None, :]).astype(np.float32), BF16)
    grid_spec = pltpu.PrefetchScalarGridSpec(
        num_scalar_prefetch=1,
        grid=(bsz, SB_HEADS // 2, t // tq),
        in_specs=[pl.BlockSpec((1, tq, LANES), lambda b, hp, i, bias: (b, i, hp)),
                  pl.BlockSpec((1, t, LANES), lambda b, hp, i, bias: (b, 0, hp)),
                  pl.BlockSpec((1, t, LANES), lambda b, hp, i, bias: (b, 0, hp)),
                  pl.BlockSpec((tk, tk), lambda b, hp, i, bias: (0, 0))],
        out_specs=pl.BlockSpec((1, tq, LANES), lambda b, hp, i, bias: (b, i, hp)),
        scratch_shapes=[pltpu.VMEM((tq, LANES), F32)] * 3,
    )
    return pl.pallas_call(
        functools.partial(_sb_prompt_kernel, tq=tq, tk=tk),
        grid_spec=grid_spec,
        out_shape=jax.ShapeDtypeStruct((bsz, t, SB_W), F32),
        compiler_params=_cparams(("parallel", "parallel", "arbitrary")),
        name="sb_prompt",
    )(bias, q, k, v, ntri)


def _sb_sample_kernel(pt_ref, bias_ref, q_ref, kn_ref, vn_ref, kc_ref, vc_ref, ntri_ref, o_ref,
                      acc_ref, car_ref, *, n_new, page):
    p = pl.program_id(1)
    ntri = ntri_ref[...]

    @pl.when(p == 0)
    def _():
        acc_ref[...] = jnp.zeros_like(acc_ref)
        car_ref[...] = jnp.zeros_like(car_ref)

    def run(k_ref, v_ref, mask):
        for h in range(SB_HEADS):
            qh = q_ref[0, h]
            kh = k_ref[0, :, h, :].astype(BF16)
            vh = v_ref[0, :, h, :].astype(BF16)
            z = _dot_nt(qh, kh) + bias_ref[h]
            sp = _softplus(z)
            if mask is not None:
                sp = jnp.where(mask, sp, 0.0)
            between = _dot(sp.astype(BF16), ntri)
            a = jnp.exp((z - sp) + between + car_ref[h])
            if mask is not None:
                a = jnp.where(mask, a, 0.0)
            acc_ref[h] += _dot(a.astype(BF16), vh)
            car_ref[h] -= jnp.broadcast_to(jnp.sum(sp, axis=-1, keepdims=True), (SUBLANES, page))

    @pl.when(p == 0)
    def _():
        t_i = lax.broadcasted_iota(jnp.int32, (SUBLANES, page), 0)
        s_i = lax.broadcasted_iota(jnp.int32, (SUBLANES, page), 1)
        run(kn_ref, vn_ref, (s_i < t_i) & (s_i < n_new))

    @pl.when(p > 0)
    def _():
        run(kc_ref, vc_ref, None)

    @pl.when(p == pl.num_programs(1) - 1)
    def _():
        o_ref[0] = acc_ref[...]


def _sb_sample(q, k_new, v_new, cache_k, cache_v, page_table, bias, n_new):
    bsz = q.shape[0]
    page = cache_k.shape[1]
    n_pages = page_table.shape[1]
    t_idx = np.arange(page)
    ntri = jnp.asarray(-(t_idx[:, None] > t_idx[None, :]).astype(np.float32), BF16)

    def cache_map(b, p, pt, bias):
        return (pt[b * n_pages + (n_pages - jnp.maximum(p, 1))], 0, 0, 0)

    per_b4 = lambda b, p, pt, bias: (b, 0, 0, 0)
    grid_spec = pltpu.PrefetchScalarGridSpec(
        num_scalar_prefetch=2,
        grid=(bsz, n_pages + 1),
        in_specs=[pl.BlockSpec((1, SB_HEADS, SUBLANES, SB_DIM), per_b4),
                  pl.BlockSpec((1, page, SB_HEADS, SB_DIM), per_b4),
                  pl.BlockSpec((1, page, SB_HEADS, SB_DIM), per_b4),
                  pl.BlockSpec((1, page, SB_HEADS, SB_DIM), cache_map),
                  pl.BlockSpec((1, page, SB_HEADS, SB_DIM), cache_map),
                  pl.BlockSpec((page, page), lambda b, p, pt, bias: (0, 0))],
        out_specs=pl.BlockSpec((1, SB_HEADS, SUBLANES, SB_DIM), per_b4),
        scratch_shapes=[pltpu.VMEM((SB_HEADS, SUBLANES, SB_DIM), F32),
                        pltpu.VMEM((SB_HEADS, SUBLANES, page), F32)],
    )
    return pl.pallas_call(
        functools.partial(_sb_sample_kernel, n_new=n_new, page=page),
        grid_spec=grid_spec,
        out_shape=jax.ShapeDtypeStruct((bsz, SB_HEADS, SUBLANES, SB_DIM), F32),
        compiler_params=_cparams(("parallel", "arbitrary")),
        name="sb_sample",
    )(page_table.reshape(-1), bias, q, k_new, v_new, cache_k, cache_v, ntri)


def _mix_out_kernel(h_ref, og_ref, gr_ref, os_ref, ggain_ref, sgain_ref, wo_ref, fgain_ref, rw_ref, rb_ref,
                    h1_ref, xn_ref, eidx_ref, gate_ref):
    og = og_ref[...]
    parts = []
    for hh in range(GLA_HEADS):
        x = og[:, hh * GLA_DV:(hh + 1) * GLA_DV]
        parts.append(x * lax.rsqrt(jnp.mean(x * x, axis=-1, keepdims=True) + EPS))
    gr = gr_ref[...]
    og_n = (jnp.concatenate(parts, axis=1) * ggain_ref[...]) * (gr * (1.0 / (1.0 + jnp.exp(-gr))))
    os_n = _half_lane_rms(os_ref[...], sgain_ref[...])
    h1 = h_ref[...] + (_dot(og_n.astype(BF16), wo_ref[:GLA_V_W, :]) + _dot(os_n.astype(BF16), wo_ref[GLA_V_W:, :]))
    h1_ref[...] = h1

    xn = (h1 * lax.rsqrt(jnp.mean(h1 * h1, axis=-1, keepdims=True) + EPS)) * fgain_ref[...]
    for j in range(xn.shape[1] // LANES):
        xn_ref[:, j, :] = xn[:, j * LANES:(j + 1) * LANES]

    x_hi, x_lo = _split_bf16(xn)
    rw = rw_ref[...]
    l2 = _dot(x_hi, rw) + _dot(x_lo, rw)
    logits = l2[:, :LANES] + l2[:, LANES:] + rb_ref[...]
    lane = _lane_iota(logits.shape).astype(F32)
    big = jnp.float32(4 * LANES)
    neg = jnp.float32(-jnp.inf)

    is_g = (lane >= N_EXPERTS) & (lane < N_EXPERTS + N_GROUPS)
    lg = jnp.where(is_g, logits, neg)
    mg = jnp.max(lg, axis=-1, keepdims=True)
    g_val = 1.0 / jnp.sum(jnp.exp(lg - mg), axis=-1, keepdims=True)
    g_idx = jnp.min(jnp.where(lg == mg, lane, big), axis=-1, keepdims=True) - N_EXPERTS

    in_g = (lane >= g_idx * EXPERTS_PER_GROUP) & (lane < (g_idx + 1) * EXPERTS_PER_GROUP)
    le = jnp.where(in_g, logits, neg)
    m1 = jnp.max(le, axis=-1, keepdims=True)
    se = jnp.sum(jnp.exp(le - m1), axis=-1, keepdims=True)
    i1 = jnp.min(jnp.where(le == m1, lane, big), axis=-1, keepdims=True)
    le2 = jnp.where(lane == i1, neg, le)
    m2 = jnp.max(le2, axis=-1, keepdims=True)
    i2 = jnp.min(jnp.where(le2 == m2, lane, big), axis=-1, keepdims=True)
    p1 = 1.0 / se
    p2 = jnp.exp(m2 - m1) / se
    tot = p1 + p2
    w1 = g_val * p1 / tot
    w2 = g_val * p2 / tot
    eidx_ref[...] = jnp.where(lane == 0, i1, jnp.where(lane == 1, i2, 0.0))[:, :SUBLANES].astype(jnp.int32)
    gate_ref[...] = jnp.where(lane == 0, w1, jnp.where(lane == 1, w2, 0.0))[:, :SUBLANES]


def _mix_out(h2, og, gr, osb, ggain, sgain, w_out_b, fgain, rw, rb):
    n, d = h2.shape
    tm = min(ROW_TILE, n)
    row = lambda w: pl.BlockSpec((tm, w), lambda i: (i, 0))
    full = lambda a: pl.BlockSpec(a.shape, lambda i: (0,) * a.ndim)
    return pl.pallas_call(
        _mix_out_kernel,
        grid=(n // tm,),
        in_specs=[row(d), row(GLA_V_W), row(GLA_V_W), row(SB_W), full(ggain), full(sgain), full(w_out_b),
                  full(fgain), full(rw), full(rb)],
        out_specs=[row(d), pl.BlockSpec((tm, d // LANES, LANES), lambda i: (i, 0, 0)), row(SUBLANES),
                   row(SUBLANES)],
        out_shape=[jax.ShapeDtypeStruct((n, d), F32), jax.ShapeDtypeStruct((n, d // LANES, LANES), F32),
                   jax.ShapeDtypeStruct((n, SUBLANES), jnp.int32), jax.ShapeDtypeStruct((n, SUBLANES), F32)],
        compiler_params=_cparams(("parallel",)),
        name="mix_out",
    )(h2, og, gr, osb, ggain, sgain, w_out_b, fgain, rw, rb)


def _moe_slots_kernel(eidx_ref, ltri_ref, dest_ref, cnt_ref, counts_ref, run_ref, pstart_ref):
    ph = pl.program_id(0)
    i = pl.program_id(1)
    e = eidx_ref[...]
    tm = e.shape[0]
    lane = _lane_iota((tm, LANES))
    oh0 = (lane == e[:, 0:1]).astype(F32)
    oh1 = (lane == e[:, 1:2]).astype(F32)
    tot0 = jnp.sum(oh0, axis=0, keepdims=True)
    tot1 = jnp.sum(oh1, axis=0, keepdims=True)

    @pl.when((ph == 0) & (i == 0))
    def _():
        counts_ref[...] = jnp.zeros_like(counts_ref)

    @pl.when(ph == 0)
    def _():
        counts_ref[...] += tot0 + tot1

    @pl.when((ph == 1) & (i == 0))
    def _():
        cnt = counts_ref[...]
        padded = jnp.floor((cnt + (MOE_BLOCK - 1)) * (1.0 / MOE_BLOCK)) * MOE_BLOCK
        x = jnp.broadcast_to(padded, (SUBLANES, LANES))
        l8 = _lane_iota((SUBLANES, LANES))
        s = 1
        while s < LANES:
            x = x + jnp.where(l8 >= s, pltpu.roll(x, s, axis=1), 0.0)
            s *= 2
        pstart_ref[...] = x[0:1] - padded
        run_ref[...] = jnp.zeros_like(run_ref)
        cnt_ref[...] = jnp.broadcast_to(cnt, (SUBLANES, LANES)).astype(jnp.int32)

    @pl.when(ph == 1)
    def _():
        ltri = ltri_ref[...]
        base0 = run_ref[...] + pstart_ref[...]
        c0 = _dot(ltri, oh0.astype(BF16))
        c1 = _dot(ltri, oh1.astype(BF16))
        d0 = jnp.sum(oh0 * (base0 + c0), axis=-1, keepdims=True)
        d1 = jnp.sum(oh1 * (base0 + tot0 + c1), axis=-1, keepdims=True)
        l8 = _lane_iota((tm, LANES))
        dest_ref[...] = jnp.where(l8 == 0, d0, jnp.where(l8 == 1, d1, 0.0))[:, :SUBLANES].astype(jnp.int32)
        run_ref[...] += tot0 + tot1


def _moe_slots(eidx):
    n = eidx.shape[0]
    tm = MOE_TILE
    t_idx = np.arange(tm)
    ltri = jnp.asarray((t_idx[None, :] < t_idx[:, None]).astype(np.float32), BF16)
    return pl.pallas_call(
        _moe_slots_kernel,
        grid=(2, n // tm),
        in_specs=[pl.BlockSpec((tm, SUBLANES), lambda ph, i: (i, 0)),
                  pl.BlockSpec((tm, tm), lambda ph, i: (0, 0))],
        out_specs=[pl.BlockSpec((tm, SUBLANES), lambda ph, i: (i * ph, 0)),
                   pl.BlockSpec((SUBLANES, LANES), lambda ph, i: (0, 0))],
        out_shape=[jax.ShapeDtypeStruct((n, SUBLANES), jnp.int32),
                   jax.ShapeDtypeStruct((SUBLANES, LANES), jnp.int32)],
        scratch_shapes=[pltpu.VMEM((1, LANES), F32)] * 3,
        compiler_params=_cparams(("arbitrary", "arbitrary")),
        name="moe_slots",
    )(eidx, ltri)


def _moe_dispatch_kernel(d0_hbm, d1_hbm, x_ref, xs_in, xs_hbm, d0_s, d1_s, isem, sem, *, tm):
    del xs_in
    i = pl.program_id(0)
    base = pl.multiple_of(i * tm, tm)
    c0 = pltpu.make_async_copy(d0_hbm.at[pl.ds(base, tm)], d0_s, isem.at[0])
    c1 = pltpu.make_async_copy(d1_hbm.at[pl.ds(base, tm)], d1_s, isem.at[1])
    c0.start()
    c1.start()
    c0.wait()
    c1.wait()

    def copies(r):
        return (pltpu.make_async_copy(x_ref.at[r], xs_hbm.at[d0_s[r]], sem.at[0]),
                pltpu.make_async_copy(x_ref.at[r], xs_hbm.at[d1_s[r]], sem.at[1]))

    def issue(r, carry):
        a, b = copies(r)
        a.start()
        b.start()
        return carry

    def drain(r, carry):
        a, b = copies(r)
        a.wait()
        b.wait()
        return carry

    lax.fori_loop(0, tm, issue, 0)
    lax.fori_loop(0, tm, drain, 0)


def _moe_dispatch(d0, d1, xn3, m_pad):
    n, s, _ = xn3.shape
    tm = MOE_TILE
    xs0 = jnp.zeros((m_pad, s, LANES), F32)
    return pl.pallas_call(
        functools.partial(_moe_dispatch_kernel, tm=tm),
        grid=(n // tm,),
        in_specs=[pl.BlockSpec(memory_space=pl.ANY), pl.BlockSpec(memory_space=pl.ANY),
                  pl.BlockSpec((tm, s, LANES), lambda i: (i, 0, 0)),
                  pl.BlockSpec(memory_space=pl.ANY)],
        out_specs=pl.BlockSpec(memory_space=pl.ANY),
        out_shape=jax.ShapeDtypeStruct((m_pad, s, LANES), F32),
        scratch_shapes=[pltpu.SMEM((tm,), jnp.int32), pltpu.SMEM((tm,), jnp.int32),
                        pltpu.SemaphoreType.DMA((2,)), pltpu.SemaphoreType.DMA((2,))],
        input_output_aliases={3: 0},
        compiler_params=_cparams(("arbitrary",)),
        name="moe_dispatch",
    )(d0, d1, xn3, xs0)


def _moe_experts_kernel(be_ref, nu_ref, xs_ref, wg_ref, wu_ref, wd_ref, y_ref):
    i = pl.program_id(0)
    ns = xs_ref.shape[1]

    @pl.when(i < nu_ref[0])
    def _():
        x = jnp.concatenate([xs_ref[:, j, :] for j in range(ns)], axis=1).astype(BF16)
        g = _dot(x, wg_ref[0])
        u = _dot(x, wu_ref[0])
        hdn = (g * (1.0 / (1.0 + jnp.exp(-g)))) * u
        y = _dot(hdn.astype(BF16), wd_ref[0])
        for j in range(ns):
            y_ref[:, j, :] = y[:, j * LANES:(j + 1) * LANES]

    @pl.when(i >= nu_ref[0])
    def _():
        y_ref[...] = jnp.zeros_like(y_ref)


def _moe_experts(block_e, n_used, xs, wg, wu, wd):
    m_pad, s, _ = xs.shape
    nb = m_pad // MOE_BLOCK
    d, de = wg.shape[1], wg.shape[2]
    rows = pl.BlockSpec((MOE_BLOCK, s, LANES), lambda i, be, nu: (i, 0, 0))
    grid_spec = pltpu.PrefetchScalarGridSpec(
        num_scalar_prefetch=2,
        grid=(nb,),
        in_specs=[rows,
                  pl.BlockSpec((1, d, de), lambda i, be, nu: (be[i], 0, 0)),
                  pl.BlockSpec((1, d, de), lambda i, be, nu: (be[i], 0, 0)),
                  pl.BlockSpec((1, de, d), lambda i, be, nu: (be[i], 0, 0))],
        out_specs=rows,
    )
    return pl.pallas_call(
        _moe_experts_kernel,
        grid_spec=grid_spec,
        out_shape=jax.ShapeDtypeStruct((m_pad, s, LANES), F32),
        compiler_params=_cparams(("arbitrary",)),
        name="moe_experts",
    )(block_e, n_used, xs, wg, wu, wd)


def _moe_combine_kernel(d0_hbm, d1_hbm, gate_ref, h1_ref, y_hbm, out_ref, d0_s, d1_s, buf0, buf1, isem, sem,
                        *, tm):
    i = pl.program_id(0)
    base = pl.multiple_of(i * tm, tm)
    c0 = pltpu.make_async_copy(d0_hbm.at[pl.ds(base, tm)], d0_s, isem.at[0])
    c1 = pltpu.make_async_copy(d1_hbm.at[pl.ds(base, tm)], d1_s, isem.at[1])
    c0.start()
    c1.start()
    c0.wait()
    c1.wait()

    def copies(r):
        return (pltpu.make_async_copy(y_hbm.at[d0_s[r]], buf0.at[r], sem.at[0]),
                pltpu.make_async_copy(y_hbm.at[d1_s[r]], buf1.at[r], sem.at[1]))

    def issue(r, carry):
        a, b = copies(r)
        a.start()
        b.start()
        return carry

    def drain(r, carry):
        a, b = copies(r)
        a.wait()
        b.wait()
        return carry

    lax.fori_loop(0, tm, issue, 0)
    lax.fori_loop(0, tm, drain, 0)
    w0 = gate_ref[:, 0:1]
    w1 = gate_ref[:, 1:2]
    for j in range(buf0.shape[1]):
        y = buf0[:, j, :] * w0 + buf1[:, j, :] * w1
        out_ref[:, j * LANES:(j + 1) * LANES] = h1_ref[:, j * LANES:(j + 1) * LANES] + y


def _moe_combine(d0, d1, gate, h1, y_rows):
    n, d = h1.shape
    s = d // LANES
    tm = MOE_TILE
    return pl.pallas_call(
        functools.partial(_moe_combine_kernel, tm=tm),
        grid=(n // tm,),
        in_specs=[pl.BlockSpec(memory_space=pl.ANY), pl.BlockSpec(memory_space=pl.ANY),
                  pl.BlockSpec((tm, SUBLANES), lambda i: (i, 0)),
                  pl.BlockSpec((tm, d), lambda i: (i, 0)),
                  pl.BlockSpec(memory_space=pl.ANY)],
        out_specs=pl.BlockSpec((tm, d), lambda i: (i, 0)),
        out_shape=jax.ShapeDtypeStruct((n, d), F32),
        scratch_shapes=[pltpu.SMEM((tm,), jnp.int32), pltpu.SMEM((tm,), jnp.int32),
                        pltpu.VMEM((tm, s, LANES), F32), pltpu.VMEM((tm, s, LANES), F32),
                        pltpu.SemaphoreType.DMA((2,)), pltpu.SemaphoreType.DMA((2,))],
        compiler_params=_cparams(("arbitrary",)),
        name="moe_combine",
    )(d0, d1, gate, h1, y_rows)


def _moe(h1, xn3, eidx, gate, wg, wu, wd):
    n = h1.shape[0]
    n_pad = -(-n // MOE_TILE) * MOE_TILE
    if n_pad != n:
        padr = lambda a: jnp.pad(a, ((0, n_pad - n),) + ((0, 0),) * (a.ndim - 1))
        h1, xn3, eidx, gate = padr(h1), padr(xn3), padr(eidx), padr(gate)
    dest, cnt = _moe_slots(eidx)
    d0, d1 = dest[:, 0], dest[:, 1]
    nb = (2 * n_pad) // MOE_BLOCK + N_EXPERTS
    counts = cnt[0, :N_EXPERTS]
    pend = jnp.cumsum((counts + MOE_BLOCK - 1) // MOE_BLOCK)
    block_e = jnp.minimum(jnp.searchsorted(pend, jnp.arange(nb, dtype=jnp.int32), side="right"),
                          N_EXPERTS - 1).astype(jnp.int32)
    n_used = pend[-1:].astype(jnp.int32)
    xs = _moe_dispatch(d0, d1, xn3, nb * MOE_BLOCK)
    y_rows = _moe_experts(block_e, n_used, xs, wg, wu, wd)
    return _moe_combine(d0, d1, gate, h1, y_rows)[:n]


def kernel(x_prompt, x_sample, cache_sb_k, cache_sb_v, state_gla, page_table, meta_tokens, norm_mix_gain, w_in, gla_w_alpha, gla_b_alpha, gla_out_gain, sb_q_gain, sb_k_gain, sb_logit_bias, sb_out_gain, w_out, norm_ffn_gain, router_group, router_group_b, router_expert, router_expert_b, w_gate, w_up, w_down):
    bsz, seq, d = x_prompt.shape
    dbs, dseq, _ = x_sample.shape
    depth = w_in.shape[0]
    page = cache_sb_k.shape[2]
    t_real = seq + N_META
    t_pad = -(-t_real // SB_TQ) * SB_TQ
    fpad = t_pad - t_real

    hp = jnp.concatenate([jnp.zeros((bsz, fpad, d), x_prompt.dtype),
                          jnp.broadcast_to(meta_tokens[None].astype(x_prompt.dtype), (bsz, N_META, d)),
                          x_prompt], axis=1).reshape(bsz * t_pad, d)
    hs = x_sample.reshape(dbs * dseq, d)

    outs = {k: [] for k in ("kp", "vp", "sp", "ks", "vs", "ss")}
    for l in range(depth):
        w = w_in[l]
        w_in_r = jnp.concatenate([w[:, :1536], w[:, 1552:3088], w[:, 1536:1552],
                                  jnp.zeros((d, _W_IN_COLS - 3088), w.dtype)], axis=1).astype(BF16)
        wa_pad = jnp.pad(gla_w_alpha[l], ((0, LANES - GLA_RANK), (0, 0)))
        ba = gla_b_alpha[l][None]
        norm_g = norm_mix_gain[l][None]
        qgain = jnp.tile(sb_q_gain[l], SB_HEADS)[None]
        kgain = jnp.tile(sb_k_gain[l], SB_HEADS)[None]
        ggain = jnp.tile(gla_out_gain[l], GLA_HEADS)[None]
        sgain = jnp.tile(sb_out_gain[l], SB_HEADS)[None]
        fgain = norm_ffn_gain[l][None]
        w_out_b = w_out[l].astype(BF16)
        r_all = jnp.concatenate([router_expert[l].transpose(1, 0, 2).reshape(d, N_EXPERTS), router_group[l],
                                 jnp.zeros((d, LANES - N_EXPERTS - N_GROUPS), F32)], axis=1)
        r_hi = r_all.astype(BF16)
        r_lo = (r_all - r_hi.astype(F32)).astype(BF16)
        rw = jnp.concatenate([r_hi, r_lo], axis=1)
        rb = jnp.concatenate([router_expert_b[l].reshape(N_EXPERTS), router_group_b[l],
                              jnp.zeros((LANES - N_EXPERTS - N_GROUPS,), F32)])[None]
        wg, wu, wd = w_gate[l].astype(BF16), w_up[l].astype(BF16), w_down[l].astype(BF16)
        bias = sb_logit_bias[l].astype(F32)

        gq, gk, gv, gr, la, qs, ks, vs, ksb, vsb = _in_proj(hp, norm_g, w_in_r, wa_pad, ba, qgain, kgain)
        b3 = lambda a: a.reshape(bsz, t_pad, a.shape[-1])
        o_g, s_p = _gla(b3(gq), b3(gk), b3(gv), b3(la), jnp.zeros((bsz, GLA_QK_W, GLA_DV), F32), fpad)
        o_s = _sb_prompt(b3(qs), b3(ksb), b3(vsb), bias)
        h1, xn3, eidx, gate = _mix_out(hp, o_g.reshape(-1, GLA_V_W), gr, o_s.reshape(-1, SB_W), ggain, sgain,
                                       w_out_b, fgain, rw, rb)
        hp = _moe(h1, xn3, eidx, gate, wg, wu, wd)
        outs["kp"].append(b3(ks)[:, fpad:].reshape(bsz, t_real, SB_HEADS, SB_DIM))
        outs["vp"].append(b3(vs)[:, fpad:].reshape(bsz, t_real, SB_HEADS, SB_DIM))
        outs["sp"].append(s_p.reshape(bsz, GLA_HEADS, GLA_DK, GLA_DV))

        gq, gk, gv, gr, la, qs, ks, vs, ksb, vsb = _in_proj(hs, norm_g, w_in_r, wa_pad, ba, qgain, kgain)
        cpad = GLA_CHUNK - dseq
        c3 = lambda a: jnp.pad(a.reshape(dbs, dseq, a.shape[-1]), ((0, 0), (cpad, 0), (0, 0)))
        o_g, s_s = _gla(c3(gq), c3(gk), c3(gv), c3(la), state_gla[l].reshape(dbs, GLA_QK_W, GLA_DV), cpad)
        o_g = o_g[:, cpad:].reshape(dbs * dseq, GLA_V_W)
        q4 = qs.reshape(dbs, dseq, SB_HEADS, SB_DIM).transpose(0, 2, 1, 3)
        q4 = jnp.pad(q4, ((0, 0), (0, 0), (0, SUBLANES - dseq), (0, 0)))
        k4 = jnp.pad(ks.reshape(dbs, dseq, SB_HEADS, SB_DIM), ((0, 0), (0, page - dseq), (0, 0), (0, 0)))
        v4 = jnp.pad(vs.reshape(dbs, dseq, SB_HEADS, SB_DIM), ((0, 0), (0, page - dseq), (0, 0), (0, 0)))
        o4 = _sb_sample(q4, k4, v4, cache_sb_k[l], cache_sb_v[l], page_table, bias, dseq)
        o_s = o4[:, :, :dseq].transpose(0, 2, 1, 3).reshape(dbs * dseq, SB_W)
        h1, xn3, eidx, gate = _mix_out(hs, o_g, gr, o_s, ggain, sgain, w_out_b, fgain, rw, rb)
        hs = _moe(h1, xn3, eidx, gate, wg, wu, wd)
        outs["ks"].append(ks.reshape(dbs, dseq, SB_HEADS, SB_DIM))
        outs["vs"].append(vs.reshape(dbs, dseq, SB_HEADS, SB_DIM))
        outs["ss"].append(s_s.reshape(dbs, GLA_HEADS, GLA_DK, GLA_DV))

    y_prompt = hp.reshape(bsz, t_pad, d)[:, fpad + N_META:]
    y_sample = hs.reshape(dbs, dseq, d)
    return (y_prompt, y_sample, jnp.stack(outs["kp"]), jnp.stack(outs["vp"]), jnp.stack(outs["sp"]),
            jnp.stack(outs["ks"]), jnp.stack(outs["vs"]), jnp.stack(outs["ss"]))
```

```python
import functools

import jax
import jax.numpy as jnp
import numpy as np
from jax import lax
from jax.experimental import pallas as pl
from jax.experimental.pallas import tpu as pltpu

F32 = jnp.float32
BF16 = jnp.bfloat16

N_META = 16
GLA_HEADS = 4
GLA_DK = 64
GLA_DV = 128
GLA_RANK = 16
GLA_TAU = 16.0
GLA_QK_W = GLA_HEADS * GLA_DK
GLA_V_W = GLA_HEADS * GLA_DV
SB_HEADS = 8
SB_DIM = 64
SB_W = SB_HEADS * SB_DIM
N_GROUPS = 4
EXPERTS_PER_GROUP = 8
N_EXPERTS = N_GROUPS * EXPERTS_PER_GROUP
EPS = 1e-6
LOG2E = 1.4426950408889634

LANES = 128
SUBLANES = 8
VMEM_LIMIT_BYTES = 56 * 1024 * 1024

ROW_TILE = 512
GLA_CHUNK = 128
SB_TQ = 768
SB_TK = 256
SB_SAMPLE_PAGES = 8
MOE_TILE = 1024
MOE_BLOCK = 256
MOE_COMBINE_ROWS = 128
SLAB = 8

_C_GQ, _C_GK, _C_GV, _C_GR, _C_SQ, _C_SK, _C_SV, _C_GA = 0, 256, 512, 1024, 1536, 2048, 2560, 3072
_W_IN_COLS = 3200


def _cparams(sem):
    return pltpu.CompilerParams(dimension_semantics=sem, vmem_limit_bytes=VMEM_LIMIT_BYTES)


def _dot(a, b):
    return jnp.dot(a, b, preferred_element_type=F32)


def _dot_nt(a, b):
    return lax.dot_general(a, b, (((1,), (1,)), ((), ())), preferred_element_type=F32)


def _split_bf16(x):
    hi = x.astype(BF16)
    lo = (x - hi.astype(F32)).astype(BF16)
    return hi, lo


def _slab(r):
    return pl.ds(pl.multiple_of(r * SLAB, SLAB), SLAB)


def _lane_iota(shape):
    return lax.broadcasted_iota(jnp.int32, shape, len(shape) - 1)


def _half_lane_rms(x, gain):
    outs = []
    for g in range(x.shape[1] // LANES):
        xg = x[:, g * LANES:(g + 1) * LANES]
        x2 = xg * xg
        low = _lane_iota(xg.shape) < SB_DIM
        s_all = jnp.sum(x2, axis=-1, keepdims=True)
        s_lo = jnp.sum(jnp.where(low, x2, 0.0), axis=-1, keepdims=True)
        ms = jnp.where(low, s_lo, s_all - s_lo) * (1.0 / SB_DIM)
        outs.append(xg * lax.rsqrt(ms + EPS))
    return jnp.concatenate(outs, axis=1) * gain


def _in_proj_kernel(x_ref, g_ref, w_ref, wa_ref, ba_ref, qgain_ref, kgain_ref,
                    gq_ref, gk_ref, gv_ref, gr_ref, la_ref, qs_ref, ks_ref, vs_ref, ksb_ref, vsb_ref):
    x = x_ref[...]
    ms = jnp.mean(x * x, axis=-1, keepdims=True)
    xn = ((x * lax.rsqrt(ms + EPS)) * g_ref[...]).astype(BF16)

    def proj(c0, width):
        return _dot(xn, w_ref[:, c0:c0 + width])

    gq_ref[...] = proj(_C_GQ, GLA_QK_W) * (GLA_DK ** -0.5)
    gk_ref[...] = proj(_C_GK, GLA_QK_W)
    gv_ref[...] = proj(_C_GV, GLA_V_W)
    gr_ref[...] = proj(_C_GR, GLA_V_W)

    ga_hi, ga_lo = _split_bf16(proj(_C_GA, LANES))
    wa_hi, wa_lo = _split_bf16(wa_ref[...])
    u = _dot(ga_hi, wa_hi) + _dot(ga_lo, wa_hi) + _dot(ga_hi, wa_lo) + ba_ref[...]
    la_ref[...] = (jnp.minimum(u, 0.0) - jnp.log(1.0 + jnp.exp(-jnp.abs(u)))) * (1.0 / GLA_TAU)

    q_s = _half_lane_rms(proj(_C_SQ, SB_W), qgain_ref[...])
    qs_ref[...] = (q_s * (SB_DIM ** -0.5 * LOG2E)).astype(BF16)
    k_s = _half_lane_rms(proj(_C_SK, SB_W), kgain_ref[...])
    ks_ref[...] = k_s
    ksb_ref[...] = k_s.astype(BF16)
    v_s = proj(_C_SV, SB_W)
    vs_ref[...] = v_s
    vsb_ref[...] = v_s.astype(BF16)


def _in_proj(h2, norm_g, w_in_r, wa_pad, ba, qgain, kgain):
    n, d = h2.shape
    tm = min(ROW_TILE, n)
    row = lambda w: pl.BlockSpec((tm, w), lambda i: (i, 0))
    full = lambda a: pl.BlockSpec(a.shape, lambda i: (0,) * a.ndim)
    outs = [(GLA_QK_W, F32), (GLA_QK_W, F32), (GLA_V_W, F32), (GLA_V_W, F32), (GLA_QK_W, F32),
            (SB_W, BF16), (SB_W, F32), (SB_W, F32), (SB_W, BF16), (SB_W, BF16)]
    return pl.pallas_call(
        _in_proj_kernel,
        grid=(n // tm,),
        in_specs=[row(d), full(norm_g), full(w_in_r), full(wa_pad), full(ba), full(qgain), full(kgain)],
        out_specs=[row(w) for w, _ in outs],
        out_shape=[jax.ShapeDtypeStruct((n, w), dt) for w, dt in outs],
        compiler_params=_cparams(("parallel",)),
        name="in_proj",
    )(h2, norm_g, w_in_r, wa_pad, ba, qgain, kgain)


def _gla_levels(c):
    levels = []
    l = c // 2
    while l >= 1:
        levels.append(l)
        l //= 2
    return levels


def _gla_constants(c):
    t = np.arange(c)
    tri = (t[None, :] <= t[:, None]).astype(np.float32)
    mats, masks = [tri], []
    for l in _gla_levels(c):
        mid = (t // (2 * l)) * (2 * l) + l
        mats.append((t[None, :] <= (mid[:, None] - 1)).astype(np.float32))
        same = (t[:, None] // (2 * l)) == (t[None, :] // (2 * l))
        masks.append((same & ((t[:, None] % (2 * l)) >= l) & ((t[None, :] % (2 * l)) < l)).astype(np.float32))
    masks.append(np.eye(c, dtype=np.float32))
    return np.concatenate(mats, axis=0), np.stack(masks)


def _gla_kernel(q_ref, k_ref, v_ref, la_ref, s0_ref, gmat_ref, mask_ref, o_ref, s_out_ref, st_ref,
                *, chunk, front_pad):
    c = pl.program_id(1)
    levels = _gla_levels(chunk)
    w = GLA_QK_W

    @pl.when(c == 0)
    def _():
        st_ref[...] = s0_ref[0].T

    q = q_ref[0]
    k = k_ref[0]
    la = la_ref[0]
    if front_pad:
        row = lax.broadcasted_iota(jnp.int32, la.shape, 0) + c * chunk
        la = jnp.where(row < front_pad, 0.0, la)
    la_hi, la_lo = _split_bf16(la)
    p = _dot(gmat_ref[...], jnp.concatenate([la_hi, la_lo], axis=1))
    p = p[:, :w] + p[:, w:]
    b = p[:chunk]
    lane = _lane_iota((1, w))
    head_masks = [(lane >= h * GLA_DK) & (lane < (h + 1) * GLA_DK) for h in range(GLA_HEADS)]

    scores = [jnp.zeros((chunk, chunk), F32) for _ in range(GLA_HEADS)]
    for i, _l in enumerate(levels):
        r = p[(i + 1) * chunk:(i + 2) * chunk]
        qt = q * jnp.exp(jnp.minimum(b - r, 0.0))
        kt = (k * jnp.exp(jnp.minimum(r - b, 0.0))).astype(BF16)
        for h in range(GLA_HEADS):
            qh = jnp.where(head_masks[h], qt, 0.0).astype(BF16)
            scores[h] = scores[h] + mask_ref[i] * _dot_nt(qh, kt)
    kb = k.astype(BF16)
    for h in range(GLA_HEADS):
        qh = jnp.where(head_masks[h], q, 0.0).astype(BF16)
        scores[h] = scores[h] + mask_ref[len(levels)] * _dot_nt(qh, kb)

    st = st_ref[...]
    st_b = st.astype(BF16)
    b_last = b[chunk - 1:chunk]
    q_in = q * jnp.exp(b)
    k_out = k * jnp.exp(b_last - b)
    upd = jnp.zeros_like(st)
    for h in range(GLA_HEADS):
        vh = v_ref[0, :, h * GLA_DV:(h + 1) * GLA_DV]
        vhb = vh.astype(BF16)
        qh = jnp.where(head_masks[h], q_in, 0.0).astype(BF16)
        o_ref[0, :, h * GLA_DV:(h + 1) * GLA_DV] = _dot(scores[h].astype(BF16), vhb) + _dot_nt(qh, st_b)
        kh = jnp.where(head_masks[h], k_out, 0.0).astype(BF16)
        upd = upd + _dot(vh.T.astype(BF16), kh)
    st_new = st * jnp.exp(b_last) + upd
    st_ref[...] = st_new

    @pl.when(c == pl.num_programs(1) - 1)
    def _():
        s_out_ref[0] = st_new.T


def _gla(q, k, v, la, s0, front_pad):
    bsz, t, _ = q.shape
    chunk = GLA_CHUNK
    gmat, masks = _gla_constants(chunk)
    gmat = jnp.asarray(gmat, BF16)
    masks = jnp.asarray(masks, F32)
    tok = lambda w: pl.BlockSpec((1, chunk, w), lambda b, c: (b, c, 0))
    per_b = pl.BlockSpec((1, GLA_QK_W, GLA_DV), lambda b, c: (b, 0, 0))
    return pl.pallas_call(
        functools.partial(_gla_kernel, chunk=chunk, front_pad=front_pad),
        grid=(bsz, t // chunk),
        in_specs=[tok(GLA_QK_W), tok(GLA_QK_W), tok(GLA_V_W), tok(GLA_QK_W), per_b,
                  pl.BlockSpec(gmat.shape, lambda b, c: (0, 0)),
                  pl.BlockSpec(masks.shape, lambda b, c: (0, 0, 0))],
        out_specs=[tok(GLA_V_W), per_b],
        out_shape=[jax.ShapeDtypeStruct((bsz, t, GLA_V_W), F32),
                   jax.ShapeDtypeStruct((bsz, GLA_QK_W, GLA_DV), F32)],
        scratch_shapes=[pltpu.VMEM((GLA_DV, GLA_QK_W), F32)],
        compiler_params=_cparams(("parallel", "arbitrary")),
        name="gla",
    )(q, k, v, la, s0, gmat, masks)


def _softplus2(z):
    neg_abs = lax.bitcast_convert_type(lax.bitcast_convert_type(z, jnp.uint32) | jnp.uint32(0x80000000), F32)
    return jnp.maximum(z, 0.0) + jnp.log2(1.0 + jnp.exp2(neg_abs))


def _sb_tile(qh, kh, vh, ntri, carry, mask):
    z = _dot_nt(qh, kh)
    sp = _softplus2(z)
    if mask is not None:
        sp = jnp.where(mask, sp, 0.0)
    between = _dot(sp.astype(BF16), ntri)
    reps = z.shape[1] // LANES
    a = jnp.exp2((z - sp) + between + jnp.concatenate([carry] * reps, axis=1))
    if mask is not None:
        a = jnp.where(mask, a, 0.0)
    return _dot(a.astype(BF16), vh), jnp.sum(sp, axis=-1, keepdims=True)


def _sb_prompt_kernel(bias_ref, q_ref, k_ref, v_ref, ntri_ref, o_ref, acc_ref, c0_ref, c1_ref, *, tq, tk):
    hp = pl.program_id(1)
    i = pl.program_id(2)
    ndiag = tq // tk
    q = q_ref[0].astype(F32)
    lane_q = _lane_iota(q.shape)
    q0 = jnp.where(lane_q < SB_DIM, q, jnp.where(lane_q < SB_DIM + 2, 1.0, 0.0)).astype(BF16)
    q1 = jnp.where(lane_q >= SB_DIM, q, jnp.where(lane_q < 2, 1.0, 0.0)).astype(BF16)
    ntri = ntri_ref[...]
    acc_ref[...] = jnp.zeros_like(acc_ref)
    c0_ref[...] = jnp.zeros_like(c0_ref)
    c1_ref[...] = jnp.zeros_like(c1_ref)
    lane_k = _lane_iota((tk, LANES))
    low_k = lane_k < SB_DIM

    def bias_lanes(h, first_lane):
        hi = jnp.full((tk, LANES), bias_ref[4 * hp + 2 * h], F32)
        lo = jnp.full((tk, LANES), bias_ref[4 * hp + 2 * h + 1], F32)
        return jnp.where(lane_k == first_lane, hi, jnp.where(lane_k == first_lane + 1, lo, 0.0)).astype(BF16)

    kbias0 = bias_lanes(0, SB_DIM)
    kbias1 = bias_lanes(1, 0)

    def step(kstart, r0, mask):
        kb = k_ref[0, pl.ds(kstart, tk), :]
        vb = v_ref[0, pl.ds(kstart, tk), :]
        zk = jnp.zeros_like(vb)
        k0 = jnp.where(low_k, kb, kbias0)
        k1 = jnp.where(low_k, kbias1, kb)
        v0 = jnp.where(low_k, vb, zk)
        v1 = jnp.where(low_k, zk, vb)
        rows = slice(r0, tq)
        o0, s0 = _sb_tile(q0[rows], k0, v0, ntri, c0_ref[rows, :], mask)
        o1, s1 = _sb_tile(q1[rows], k1, v1, ntri, c1_ref[rows, :], mask)
        acc_ref[rows, :] += o0 + o1
        c0_ref[rows, :] -= jnp.broadcast_to(s0, (tq - r0, LANES))
        c1_ref[rows, :] -= jnp.broadcast_to(s1, (tq - r0, LANES))

    for d in reversed(range(ndiag)):
        nr = tq - d * tk
        mask = lax.broadcasted_iota(jnp.int32, (nr, tk), 0) > lax.broadcasted_iota(jnp.int32, (nr, tk), 1)
        step(pl.multiple_of(i * tq + d * tk, tk), d * tk, mask)

    nfull = i * ndiag

    def body(it, carry):
        step(pl.multiple_of((nfull - 1 - it) * tk, tk), 0, None)
        return carry

    lax.fori_loop(0, nfull, body, 0)
    o_ref[0] = acc_ref[...]


def _sb_prompt(q, k, v, bias):
    bsz, t, _ = q.shape
    tq, tk = SB_TQ, SB_TK
    t_idx = np.arange(tk)
    ntri = jnp.asarray(-(t_idx[:, None] > t_idx[None, :]).astype(np.float32), BF16)
    grid_spec = pltpu.PrefetchScalarGridSpec(
        num_scalar_prefetch=1,
        grid=(bsz, SB_HEADS // 2, t // tq),
        in_specs=[pl.BlockSpec((1, tq, LANES), lambda b, hp, i, bias: (b, i, hp)),
                  pl.BlockSpec((1, t, LANES), lambda b, hp, i, bias: (b, 0, hp)),
                  pl.BlockSpec((1, t, LANES), lambda b, hp, i, bias: (b, 0, hp)),
                  pl.BlockSpec((tk, tk), lambda b, hp, i, bias: (0, 0))],
        out_specs=pl.BlockSpec((1, tq, LANES), lambda b, hp, i, bias: (b, i, hp)),
        scratch_shapes=[pltpu.VMEM((tq, LANES), F32)] * 3,
    )
    return pl.pallas_call(
        functools.partial(_sb_prompt_kernel, tq=tq, tk=tk),
        grid_spec=grid_spec,
        out_shape=jax.ShapeDtypeStruct((bsz, t, SB_W), F32),
        compiler_params=_cparams(("parallel", "parallel", "arbitrary")),
        name="sb_prompt",
    )(bias, q, k, v, ntri)


def _sb_sample_kernel(pt_ref, q_ref, bias_ref, kn_ref, vn_ref, *rest, n_new, page, pages_per_step):
    kc_refs = rest[:pages_per_step]
    vc_refs = rest[pages_per_step:2 * pages_per_step]
    ntri_ref, o_ref, acc_ref, car_ref = rest[2 * pages_per_step:]
    j = pl.program_id(1)
    rows = SB_HEADS * SUBLANES
    q = q_ref[0]
    bias = bias_ref[...]
    ntri = ntri_ref[...]

    def run(k_refs, v_refs, mask):
        n = len(k_refs)
        kt = jnp.concatenate([r[0].reshape(SB_W, page).astype(BF16) for r in k_refs], axis=1)
        vt = jnp.concatenate([r[0].reshape(SB_W, page).astype(BF16) for r in v_refs], axis=1)
        z_all = _dot(q, kt)
        zs, sps = [], []
        for r in range(n):
            z = z_all[:, r * page:(r + 1) * page] + bias
            sp = _softplus2(z)
            if mask is not None:
                sp = jnp.where(mask, sp, 0.0)
            zs.append(z)
            sps.append(sp)
        between = _dot(jnp.concatenate(sps, axis=0).astype(BF16), ntri)
        car = car_ref[...]
        a_list = []
        for r in range(n):
            a = jnp.exp2((zs[r] - sps[r]) + between[r * rows:(r + 1) * rows] + car)
            if mask is not None:
                a = jnp.where(mask, a, 0.0)
            a_list.append(a.astype(BF16))
            car = car - jnp.broadcast_to(jnp.sum(sps[r], axis=-1, keepdims=True), car.shape)
        acc_ref[...] += _dot_nt(jnp.concatenate(a_list, axis=1), vt)
        car_ref[...] = car

    @pl.when(j == 0)
    def _():
        acc_ref[...] = jnp.zeros_like(acc_ref)
        car_ref[...] = jnp.zeros_like(car_ref)
        t_i = lax.broadcasted_iota(jnp.int32, (rows, page), 0) % SUBLANES
        s_i = lax.broadcasted_iota(jnp.int32, (rows, page), 1)
        run([kn_ref], [vn_ref], (s_i < t_i) & (s_i < n_new))

    run(kc_refs, vc_refs, None)

    @pl.when(j == pl.num_programs(1) - 1)
    def _():
        acc = acc_ref[...]
        for h in range(SB_HEADS):
            o_ref[0, h] = acc[h * SUBLANES:(h + 1) * SUBLANES, h * SB_DIM:(h + 1) * SB_DIM]


def _sb_sample(q_bd, bias_rows, kt_new, vt_new, cache_kt, cache_vt, page_table, n_new):
    bsz = q_bd.shape[0]
    page = cache_kt.shape[3]
    n_pages = page_table.shape[1]
    pps = max(p for p in range(1, SB_SAMPLE_PAGES + 1) if n_pages % p == 0)
    rows = SB_HEADS * SUBLANES
    t_idx = np.arange(page)
    ntri = jnp.asarray(-(t_idx[:, None] > t_idx[None, :]).astype(np.float32), BF16)

    def cache_map(r):
        return lambda b, j, pt: (pt[b * n_pages + (n_pages - 1 - (j * pps + r))], 0, 0, 0)

    per_b3 = lambda b, j, pt: (b, 0, 0)
    per_b4 = lambda b, j, pt: (b, 0, 0, 0)
    page_block = (1, SB_HEADS, SB_DIM, page)
    grid_spec = pltpu.PrefetchScalarGridSpec(
        num_scalar_prefetch=1,
        grid=(bsz, n_pages // pps),
        in_specs=[pl.BlockSpec((1, rows, SB_W), per_b3),
                  pl.BlockSpec((rows, page), lambda b, j, pt: (0, 0)),
                  pl.BlockSpec(page_block, per_b4),
                  pl.BlockSpec(page_block, per_b4)]
                 + [pl.BlockSpec(page_block, cache_map(r)) for r in range(pps)]
                 + [pl.BlockSpec(page_block, cache_map(r)) for r in range(pps)]
                 + [pl.BlockSpec((page, page), lambda b, j, pt: (0, 0))],
        out_specs=pl.BlockSpec((1, SB_HEADS, SUBLANES, SB_DIM), per_b4),
        scratch_shapes=[pltpu.VMEM((rows, SB_W), F32), pltpu.VMEM((rows, page), F32)],
    )
    return pl.pallas_call(
        functools.partial(_sb_sample_kernel, n_new=n_new, page=page, pages_per_step=pps),
        grid_spec=grid_spec,
        out_shape=jax.ShapeDtypeStruct((bsz, SB_HEADS, SUBLANES, SB_DIM), F32),
        compiler_params=_cparams(("parallel", "arbitrary")),
        name="sb_sample",
    )(page_table.reshape(-1), q_bd, bias_rows, kt_new, vt_new, *([cache_kt] * pps), *([cache_vt] * pps), ntri)


def _mix_out_kernel(h_ref, og_ref, gr_ref, os_ref, ggain_ref, sgain_ref, wo_ref, fgain_ref, rw_ref, rb_ref,
                    h1_ref, xn_ref, eidx_ref, gate_ref):
    og = og_ref[...]
    parts = []
    for hh in range(GLA_HEADS):
        x = og[:, hh * GLA_DV:(hh + 1) * GLA_DV]
        parts.append(x * lax.rsqrt(jnp.mean(x * x, axis=-1, keepdims=True) + EPS))
    gr = gr_ref[...]
    og_n = (jnp.concatenate(parts, axis=1) * ggain_ref[...]) * (gr * (1.0 / (1.0 + jnp.exp(-gr))))
    os_n = _half_lane_rms(os_ref[...], sgain_ref[...])
    h1 = h_ref[...] + (_dot(og_n.astype(BF16), wo_ref[:GLA_V_W, :]) + _dot(os_n.astype(BF16), wo_ref[GLA_V_W:, :]))
    h1_ref[...] = h1

    xn = (h1 * lax.rsqrt(jnp.mean(h1 * h1, axis=-1, keepdims=True) + EPS)) * fgain_ref[...]
    for j in range(xn.shape[1] // LANES):
        xn_ref[pl.ds(j, xn.shape[0], stride=SLAB), :] = xn[:, j * LANES:(j + 1) * LANES]

    x_hi, x_lo = _split_bf16(xn)
    rw = rw_ref[...]
    l2 = _dot(x_hi, rw) + _dot(x_lo, rw)
    logits = l2[:, :LANES] + l2[:, LANES:] + rb_ref[...]
    lane = _lane_iota(logits.shape).astype(F32)
    big = jnp.float32(4 * LANES)
    neg = jnp.float32(-jnp.inf)

    is_g = (lane >= N_EXPERTS) & (lane < N_EXPERTS + N_GROUPS)
    lg = jnp.where(is_g, logits, neg)
    mg = jnp.max(lg, axis=-1, keepdims=True)
    g_val = 1.0 / jnp.sum(jnp.exp(lg - mg), axis=-1, keepdims=True)
    g_idx = jnp.min(jnp.where(lg == mg, lane, big), axis=-1, keepdims=True) - N_EXPERTS

    in_g = (lane >= g_idx * EXPERTS_PER_GROUP) & (lane < (g_idx + 1) * EXPERTS_PER_GROUP)
    le = jnp.where(in_g, logits, neg)
    m1 = jnp.max(le, axis=-1, keepdims=True)
    se = jnp.sum(jnp.exp(le - m1), axis=-1, keepdims=True)
    i1 = jnp.min(jnp.where(le == m1, lane, big), axis=-1, keepdims=True)
    le2 = jnp.where(lane == i1, neg, le)
    m2 = jnp.max(le2, axis=-1, keepdims=True)
    i2 = jnp.min(jnp.where(le2 == m2, lane, big), axis=-1, keepdims=True)
    p1 = 1.0 / se
    p2 = jnp.exp(m2 - m1) / se
    tot = p1 + p2
    w1 = g_val * p1 / tot
    w2 = g_val * p2 / tot
    eidx_ref[...] = jnp.where(lane == 0, i1, jnp.where(lane == 1, i2, 0.0))[:, :SUBLANES].astype(jnp.int32)
    gate_ref[...] = jnp.where(lane == 0, w1, jnp.where(lane == 1, w2, 0.0))[:, :SUBLANES]


def _mix_out(h2, og, gr, osb, ggain, sgain, w_out_b, fgain, rw, rb):
    n, d = h2.shape
    tm = min(ROW_TILE, n)
    row = lambda w: pl.BlockSpec((tm, w), lambda i: (i, 0))
    full = lambda a: pl.BlockSpec(a.shape, lambda i: (0,) * a.ndim)
    return pl.pallas_call(
        _mix_out_kernel,
        grid=(n // tm,),
        in_specs=[row(d), row(GLA_V_W), row(GLA_V_W), row(SB_W), full(ggain), full(sgain), full(w_out_b),
                  full(fgain), full(rw), full(rb)],
        out_specs=[row(d), pl.BlockSpec((tm * SLAB, LANES), lambda i: (i, 0)), row(SUBLANES),
                   row(SUBLANES)],
        out_shape=[jax.ShapeDtypeStruct((n, d), F32), jax.ShapeDtypeStruct((n * SLAB, LANES), F32),
                   jax.ShapeDtypeStruct((n, SUBLANES), jnp.int32), jax.ShapeDtypeStruct((n, SUBLANES), F32)],
        compiler_params=_cparams(("parallel",)),
        name="mix_out",
    )(h2, og, gr, osb, ggain, sgain, w_out_b, fgain, rw, rb)


def _moe_slots_kernel(eidx_ref, ltri_ref, dest_ref, cnt_ref, counts_ref, run_ref, pstart_ref):
    ph = pl.program_id(0)
    i = pl.program_id(1)
    e = eidx_ref[...]
    tm = e.shape[0]
    lane = _lane_iota((tm, LANES))
    oh0 = (lane == e[:, 0:1]).astype(F32)
    oh1 = (lane == e[:, 1:2]).astype(F32)
    tot0 = jnp.sum(oh0, axis=0, keepdims=True)
    tot1 = jnp.sum(oh1, axis=0, keepdims=True)

    @pl.when((ph == 0) & (i == 0))
    def _():
        counts_ref[...] = jnp.zeros_like(counts_ref)

    @pl.when(ph == 0)
    def _():
        counts_ref[...] += tot0 + tot1

    @pl.when((ph == 1) & (i == 0))
    def _():
        cnt = counts_ref[...]
        padded = jnp.floor((cnt + (MOE_BLOCK - 1)) * (1.0 / MOE_BLOCK)) * MOE_BLOCK
        x = jnp.broadcast_to(padded, (SUBLANES, LANES))
        l8 = _lane_iota((SUBLANES, LANES))
        s = 1
        while s < LANES:
            x = x + jnp.where(l8 >= s, pltpu.roll(x, s, axis=1), 0.0)
            s *= 2
        pstart_ref[...] = x[0:1] - padded
        run_ref[...] = jnp.zeros_like(run_ref)
        cnt_ref[...] = jnp.broadcast_to(cnt, (SUBLANES, LANES)).astype(jnp.int32)

    @pl.when(ph == 1)
    def _():
        ltri = ltri_ref[...]
        base0 = run_ref[...] + pstart_ref[...]
        c0 = _dot(ltri, oh0.astype(BF16))
        c1 = _dot(ltri, oh1.astype(BF16))
        d0 = jnp.sum(oh0 * (base0 + c0), axis=-1, keepdims=True)
        d1 = jnp.sum(oh1 * (base0 + tot0 + c1), axis=-1, keepdims=True)
        l8 = _lane_iota((tm, LANES))
        dest_ref[...] = jnp.where(l8 == 0, d0, jnp.where(l8 == 1, d1, 0.0))[:, :SUBLANES].astype(jnp.int32)
        run_ref[...] += tot0 + tot1


def _moe_slots(eidx):
    n = eidx.shape[0]
    tm = MOE_TILE
    t_idx = np.arange(tm)
    ltri = jnp.asarray((t_idx[None, :] < t_idx[:, None]).astype(np.float32), BF16)
    return pl.pallas_call(
        _moe_slots_kernel,
        grid=(2, n // tm),
        in_specs=[pl.BlockSpec((tm, SUBLANES), lambda ph, i: (i, 0)),
                  pl.BlockSpec((tm, tm), lambda ph, i: (0, 0))],
        out_specs=[pl.BlockSpec((tm, SUBLANES), lambda ph, i: (i * ph, 0)),
                   pl.BlockSpec((SUBLANES, LANES), lambda ph, i: (0, 0))],
        out_shape=[jax.ShapeDtypeStruct((n, SUBLANES), jnp.int32),
                   jax.ShapeDtypeStruct((SUBLANES, LANES), jnp.int32)],
        scratch_shapes=[pltpu.VMEM((1, LANES), F32)] * 3,
        compiler_params=_cparams(("arbitrary", "arbitrary")),
        name="moe_slots",
    )(eidx, ltri)


def _moe_dispatch_kernel(d0_hbm, d1_hbm, x_ref, xs_in, xs_hbm, d0_s, d1_s, isem, sem, *, tm):
    del xs_in
    i = pl.program_id(0)
    base = pl.multiple_of(i * tm, tm)
    c0 = pltpu.make_async_copy(d0_hbm.at[pl.ds(base, tm)], d0_s, isem.at[0])
    c1 = pltpu.make_async_copy(d1_hbm.at[pl.ds(base, tm)], d1_s, isem.at[1])
    c0.start()
    c1.start()
    c0.wait()
    c1.wait()

    def copies(r):
        src = x_ref.at[_slab(r)]
        return (pltpu.make_async_copy(src, xs_hbm.at[_slab(d0_s[r])], sem.at[0]),
                pltpu.make_async_copy(src, xs_hbm.at[_slab(d1_s[r])], sem.at[1]))

    def issue(r, carry):
        a, b = copies(r)
        a.start()
        b.start()
        return carry

    def drain(r, carry):
        a, b = copies(r)
        a.wait()
        b.wait()
        return carry

    lax.fori_loop(0, tm, issue, 0)
    lax.fori_loop(0, tm, drain, 0)


def _moe_dispatch(d0, d1, xn_slab, m_pad):
    n = xn_slab.shape[0] // SLAB
    tm = MOE_TILE
    xs0 = jnp.zeros((m_pad * SLAB, LANES), F32)
    return pl.pallas_call(
        functools.partial(_moe_dispatch_kernel, tm=tm),
        grid=(n // tm,),
        in_specs=[pl.BlockSpec(memory_space=pl.ANY), pl.BlockSpec(memory_space=pl.ANY),
                  pl.BlockSpec((tm * SLAB, LANES), lambda i: (i, 0)),
                  pl.BlockSpec(memory_space=pl.ANY)],
        out_specs=pl.BlockSpec(memory_space=pl.ANY),
        out_shape=jax.ShapeDtypeStruct((m_pad * SLAB, LANES), F32),
        scratch_shapes=[pltpu.SMEM((tm,), jnp.int32), pltpu.SMEM((tm,), jnp.int32),
                        pltpu.SemaphoreType.DMA((2,)), pltpu.SemaphoreType.DMA((2,))],
        input_output_aliases={3: 0},
        compiler_params=_cparams(("arbitrary",)),
        name="moe_dispatch",
    )(d0, d1, xn_slab, xs0)


def _moe_experts_kernel(be_ref, nu_ref, xs_ref, wg_ref, wu_ref, wd_ref, y_ref):
    i = pl.program_id(0)
    rows = xs_ref.shape[0] // SLAB

    @pl.when(i < nu_ref[0])
    def _():
        x = jnp.concatenate([xs_ref[pl.ds(j, rows, stride=SLAB), :] for j in range(SLAB)], axis=1).astype(BF16)
        g = _dot(x, wg_ref[0])
        u = _dot(x, wu_ref[0])
        hdn = (g * (1.0 / (1.0 + jnp.exp(-g)))) * u
        y = _dot(hdn.astype(BF16), wd_ref[0])
        for j in range(SLAB):
            y_ref[pl.ds(j, rows, stride=SLAB), :] = y[:, j * LANES:(j + 1) * LANES]

    @pl.when(i >= nu_ref[0])
    def _():
        y_ref[...] = jnp.zeros_like(y_ref)


def _moe_experts(block_e, n_used, xs, wg, wu, wd):
    m_pad = xs.shape[0] // SLAB
    nb = m_pad // MOE_BLOCK
    d, de = wg.shape[1], wg.shape[2]
    rows = pl.BlockSpec((MOE_BLOCK * SLAB, LANES), lambda i, be, nu: (i, 0))
    grid_spec = pltpu.PrefetchScalarGridSpec(
        num_scalar_prefetch=2,
        grid=(nb,),
        in_specs=[rows,
                  pl.BlockSpec((1, d, de), lambda i, be, nu: (be[i], 0, 0)),
                  pl.BlockSpec((1, d, de), lambda i, be, nu: (be[i], 0, 0)),
                  pl.BlockSpec((1, de, d), lambda i, be, nu: (be[i], 0, 0))],
        out_specs=rows,
    )
    return pl.pallas_call(
        _moe_experts_kernel,
        grid_spec=grid_spec,
        out_shape=jax.ShapeDtypeStruct((m_pad * SLAB, LANES), F32),
        compiler_params=_cparams(("arbitrary",)),
        name="moe_experts",
    )(block_e, n_used, xs, wg, wu, wd)


def _moe_combine_kernel(d0_hbm, d1_hbm, gate_ref, h1_ref, y_hbm, out_ref, d0_s, d1_s, buf0, buf1, isem, sem,
                        *, tm):
    i = pl.program_id(0)
    base = pl.multiple_of(i * tm, tm)
    c0 = pltpu.make_async_copy(d0_hbm.at[pl.ds(base, tm)], d0_s, isem.at[0])
    c1 = pltpu.make_async_copy(d1_hbm.at[pl.ds(base, tm)], d1_s, isem.at[1])
    c0.start()
    c1.start()
    c0.wait()
    c1.wait()

    def copies(r):
        return (pltpu.make_async_copy(y_hbm.at[_slab(d0_s[r])], buf0.at[_slab(r)], sem.at[0]),
                pltpu.make_async_copy(y_hbm.at[_slab(d1_s[r])], buf1.at[_slab(r)], sem.at[1]))

    def issue(r, carry):
        a, b = copies(r)
        a.start()
        b.start()
        return carry

    def drain(r, carry):
        a, b = copies(r)
        a.wait()
        b.wait()
        return carry

    lax.fori_loop(0, tm, issue, 0)
    lax.fori_loop(0, tm, drain, 0)
    cr = MOE_COMBINE_ROWS

    def chunk(c, carry):
        r0 = pl.multiple_of(c * cr, cr)
        rows = pl.ds(r0, cr)
        w0 = jnp.broadcast_to(gate_ref[rows, 0:1], (cr, LANES))
        w1 = jnp.broadcast_to(gate_ref[rows, 1:2], (cr, LANES))
        for j in range(SLAB):
            srows = pl.ds(r0 * SLAB + j, cr, stride=SLAB)
            y = buf0[srows, :] * w0 + buf1[srows, :] * w1
            out_ref[rows, j * LANES:(j + 1) * LANES] = h1_ref[rows, j * LANES:(j + 1) * LANES] + y
        return carry

    lax.fori_loop(0, tm // cr, chunk, 0)


def _moe_combine(d0, d1, gate, h1, y_rows):
    n, d = h1.shape
    tm = MOE_TILE
    return pl.pallas_call(
        functools.partial(_moe_combine_kernel, tm=tm),
        grid=(n // tm,),
        in_specs=[pl.BlockSpec(memory_space=pl.ANY), pl.BlockSpec(memory_space=pl.ANY),
                  pl.BlockSpec((tm, SUBLANES), lambda i: (i, 0)),
                  pl.BlockSpec((tm, d), lambda i: (i, 0)),
                  pl.BlockSpec(memory_space=pl.ANY)],
        out_specs=pl.BlockSpec((tm, d), lambda i: (i, 0)),
        out_shape=jax.ShapeDtypeStruct((n, d), F32),
        scratch_shapes=[pltpu.SMEM((tm,), jnp.int32), pltpu.SMEM((tm,), jnp.int32),
                        pltpu.VMEM((tm * SLAB, LANES), F32), pltpu.VMEM((tm * SLAB, LANES), F32),
                        pltpu.SemaphoreType.DMA((2,)), pltpu.SemaphoreType.DMA((2,))],
        compiler_params=_cparams(("arbitrary",)),
        name="moe_combine",
    )(d0, d1, gate, h1, y_rows)


def _moe(h1, xn_slab, eidx, gate, wg, wu, wd):
    n = h1.shape[0]
    assert h1.shape[1] == SLAB * LANES
    n_pad = -(-n // MOE_TILE) * MOE_TILE
    if n_pad != n:
        padr = lambda a, k: jnp.pad(a, ((0, k * (n_pad - n)),) + ((0, 0),) * (a.ndim - 1))
        h1, xn_slab, eidx, gate = padr(h1, 1), padr(xn_slab, SLAB), padr(eidx, 1), padr(gate, 1)
    dest, cnt = _moe_slots(eidx)
    d0, d1 = dest[:, 0], dest[:, 1]
    nb = (2 * n_pad) // MOE_BLOCK + N_EXPERTS
    counts = cnt[0, :N_EXPERTS]
    pend = jnp.cumsum((counts + MOE_BLOCK - 1) // MOE_BLOCK)
    block_e = jnp.minimum(jnp.sum(pend[None, :] <= jnp.arange(nb, dtype=jnp.int32)[:, None], axis=1),
                          N_EXPERTS - 1).astype(jnp.int32)
    n_used = pend[-1:].astype(jnp.int32)
    xs = _moe_dispatch(d0, d1, xn_slab, nb * MOE_BLOCK)
    y_rows = _moe_experts(block_e, n_used, xs, wg, wu, wd)
    return _moe_combine(d0, d1, gate, h1, y_rows)[:n]


def kernel(x_prompt, x_sample, cache_sb_k, cache_sb_v, state_gla, page_table, meta_tokens, norm_mix_gain, w_in, gla_w_alpha, gla_b_alpha, gla_out_gain, sb_q_gain, sb_k_gain, sb_logit_bias, sb_out_gain, w_out, norm_ffn_gain, router_group, router_group_b, router_expert, router_expert_b, w_gate, w_up, w_down):
    bsz, seq, d = x_prompt.shape
    dbs, dseq, _ = x_sample.shape
    depth = w_in.shape[0]
    page = cache_sb_k.shape[2]
    t_real = seq + N_META
    t_pad = -(-t_real // SB_TQ) * SB_TQ
    fpad = t_pad - t_real

    hp = jnp.concatenate([jnp.zeros((bsz, fpad, d), x_prompt.dtype),
                          jnp.broadcast_to(meta_tokens[None].astype(x_prompt.dtype), (bsz, N_META, d)),
                          x_prompt], axis=1).reshape(bsz * t_pad, d)
    hs = x_sample.reshape(dbs * dseq, d)

    outs = {k: [] for k in ("kp", "vp", "sp", "ks", "vs", "ss")}
    for l in range(depth):
        w = w_in[l]
        w_in_r = jnp.concatenate([w[:, :1536], w[:, 1552:3088], w[:, 1536:1552],
                                  jnp.zeros((d, _W_IN_COLS - 3088), w.dtype)], axis=1).astype(BF16)
        wa_pad = jnp.pad(gla_w_alpha[l], ((0, LANES - GLA_RANK), (0, 0)))
        ba = gla_b_alpha[l][None]
        norm_g = norm_mix_gain[l][None]
        qgain = jnp.tile(sb_q_gain[l], SB_HEADS)[None]
        kgain = jnp.tile(sb_k_gain[l], SB_HEADS)[None]
        ggain = jnp.tile(gla_out_gain[l], GLA_HEADS)[None]
        sgain = jnp.tile(sb_out_gain[l], SB_HEADS)[None]
        fgain = norm_ffn_gain[l][None]
        w_out_b = w_out[l].astype(BF16)
        r_all = jnp.concatenate([router_expert[l].transpose(1, 0, 2).reshape(d, N_EXPERTS), router_group[l],
                                 jnp.zeros((d, LANES - N_EXPERTS - N_GROUPS), F32)], axis=1)
        r_hi = r_all.astype(BF16)
        r_lo = (r_all - r_hi.astype(F32)).astype(BF16)
        rw = jnp.concatenate([r_hi, r_lo], axis=1)
        rb = jnp.concatenate([router_expert_b[l].reshape(N_EXPERTS), router_group_b[l],
                              jnp.zeros((LANES - N_EXPERTS - N_GROUPS,), F32)])[None]
        wg, wu, wd = w_gate[l].astype(BF16), w_up[l].astype(BF16), w_down[l].astype(BF16)
        bias2 = sb_logit_bias[l].astype(F32) * LOG2E
        b_hi = bias2.astype(BF16).astype(F32)
        b_lo = (bias2 - b_hi).astype(BF16).astype(F32)
        bias_hl = jnp.stack([b_hi, b_lo], axis=1).reshape(-1)

        gq, gk, gv, gr, la, qs, ks, vs, ksb, vsb = _in_proj(hp, norm_g, w_in_r, wa_pad, ba, qgain, kgain)
        b3 = lambda a: a.reshape(bsz, t_pad, a.shape[-1])
        o_g, s_p = _gla(b3(gq), b3(gk), b3(gv), b3(la), jnp.zeros((bsz, GLA_QK_W, GLA_DV), F32), fpad)
        o_s = _sb_prompt(b3(qs), b3(ksb), b3(vsb), bias_hl)
        h1, xn3, eidx, gate = _mix_out(hp, o_g.reshape(-1, GLA_V_W), gr, o_s.reshape(-1, SB_W), ggain, sgain,
                                       w_out_b, fgain, rw, rb)
        hp = _moe(h1, xn3, eidx, gate, wg, wu, wd)
        outs["kp"].append(b3(ks)[:, fpad:].reshape(bsz, t_real, SB_HEADS, SB_DIM))
        outs["vp"].append(b3(vs)[:, fpad:].reshape(bsz, t_real, SB_HEADS, SB_DIM))
        outs["sp"].append(s_p.reshape(bsz, GLA_HEADS, GLA_DK, GLA_DV))

        gq, gk, gv, gr, la, qs, ks, vs, ksb, vsb = _in_proj(hs, norm_g, w_in_r, wa_pad, ba, qgain, kgain)
        cpad = GLA_CHUNK - dseq
        c3 = lambda a: jnp.pad(a.reshape(dbs, dseq, a.shape[-1]), ((0, 0), (cpad, 0), (0, 0)))
        o_g, s_s = _gla(c3(gq), c3(gk), c3(gv), c3(la), state_gla[l].reshape(dbs, GLA_QK_W, GLA_DV), cpad)
        o_g = o_g[:, cpad:].reshape(dbs * dseq, GLA_V_W)
        q4 = qs.reshape(dbs, dseq, SB_HEADS, SB_DIM).transpose(0, 2, 1, 3)
        q4 = jnp.pad(q4, ((0, 0), (0, 0), (0, SUBLANES - dseq), (0, 0)))
        eye = jnp.eye(SB_HEADS, dtype=q4.dtype)
        q_bd = (q4[:, :, :, None, :] * eye[None, :, None, :, None]).reshape(dbs, SB_HEADS * SUBLANES, SB_W)
        bias_rows = jnp.broadcast_to(jnp.repeat(bias2, SUBLANES)[:, None], (SB_HEADS * SUBLANES, page))
        to_t = lambda a: jnp.pad(a.reshape(dbs, dseq, SB_HEADS, SB_DIM).transpose(0, 2, 3, 1),
                                 ((0, 0), (0, 0), (0, 0), (0, page - dseq)))
        cache_kt = cache_sb_k[l].transpose(0, 2, 3, 1)
        cache_vt = cache_sb_v[l].transpose(0, 2, 3, 1)
        o4 = _sb_sample(q_bd, bias_rows, to_t(ks), to_t(vs), cache_kt, cache_vt, page_table, dseq)
        o_s = o4[:, :, :dseq].transpose(0, 2, 1, 3).reshape(dbs * dseq, SB_W)
        h1, xn3, eidx, gate = _mix_out(hs, o_g, gr, o_s, ggain, sgain, w_out_b, fgain, rw, rb)
        hs = _moe(h1, xn3, eidx, gate, wg, wu, wd)
        outs["ks"].append(ks.reshape(dbs, dseq, SB_HEADS, SB_DIM))
        outs["vs"].append(vs.reshape(dbs, dseq, SB_HEADS, SB_DIM))
        outs["ss"].append(s_s.reshape(dbs, GLA_HEADS, GLA_DK, GLA_DV))

    y_prompt = hp.reshape(bsz, t_pad, d)[:, fpad + N_META:]
    y_sample = hs.reshape(dbs, dseq, d)
    return (y_prompt, y_sample, jnp.stack(outs["kp"]), jnp.stack(outs["vp"]), jnp.stack(outs["sp"]),
            jnp.stack(outs["ks"]), jnp.stack(outs["vs"]), jnp.stack(outs["ss"]))
```

```python
import functools

import jax
import jax.numpy as jnp
import numpy as np
from jax import lax
from jax.experimental import pallas as pl
from jax.experimental.pallas import tpu as pltpu

F32 = jnp.float32
BF16 = jnp.bfloat16

N_META = 16
GLA_HEADS = 4
GLA_DK = 64
GLA_DV = 128
GLA_RANK = 16
GLA_TAU = 16.0
GLA_QK_W = GLA_HEADS * GLA_DK
GLA_V_W = GLA_HEADS * GLA_DV
SB_HEADS = 8
SB_DIM = 64
SB_W = SB_HEADS * SB_DIM
N_GROUPS = 4
EXPERTS_PER_GROUP = 8
N_EXPERTS = N_GROUPS * EXPERTS_PER_GROUP
EPS = 1e-6
LOG2E = 1.4426950408889634

LANES = 128
SUBLANES = 8
VMEM_LIMIT_BYTES = 56 * 1024 * 1024

ROW_TILE = 512
GLA_CHUNK = 128
SB_TQ = 768
SB_TK = 256
SB_SAMPLE_PAGES = 8
MOE_TILE = 1024
MOE_BLOCK = 256
MOE_COMBINE_ROWS = 128
DMA_LOOP_UNROLL = 8
SLAB = 8

_C_GQ, _C_GK, _C_GV, _C_GR, _C_SQ, _C_SK, _C_SV, _C_GA = 0, 256, 512, 1024, 1536, 2048, 2560, 3072
_W_IN_COLS = 3200


def _cparams(sem):
    return pltpu.CompilerParams(dimension_semantics=sem, vmem_limit_bytes=VMEM_LIMIT_BYTES)


def _dot(a, b):
    return jnp.dot(a, b, preferred_element_type=F32)


def _dot_nt(a, b):
    return lax.dot_general(a, b, (((1,), (1,)), ((), ())), preferred_element_type=F32)


def _split_bf16(x):
    hi = x.astype(BF16)
    lo = (x - hi.astype(F32)).astype(BF16)
    return hi, lo


def _slab(r):
    return pl.ds(pl.multiple_of(r * SLAB, SLAB), SLAB)


def _lane_iota(shape):
    return lax.broadcasted_iota(jnp.int32, shape, len(shape) - 1)


def _half_lane_rms(x, gain):
    outs = []
    for g in range(x.shape[1] // LANES):
        xg = x[:, g * LANES:(g + 1) * LANES]
        x2 = xg * xg
        low = _lane_iota(xg.shape) < SB_DIM
        s_all = jnp.sum(x2, axis=-1, keepdims=True)
        s_lo = jnp.sum(jnp.where(low, x2, 0.0), axis=-1, keepdims=True)
        ms = jnp.where(low, s_lo, s_all - s_lo) * (1.0 / SB_DIM)
        outs.append(xg * lax.rsqrt(ms + EPS))
    return jnp.concatenate(outs, axis=1) * gain


def _in_proj_kernel(x_ref, g_ref, w_ref, wa_ref, ba_ref, qgain_ref, kgain_ref,
                    gq_ref, gk_ref, gv_ref, gr_ref, la_ref, qs_ref, ks_ref, vs_ref, ksb_ref, vsb_ref):
    x = x_ref[...]
    ms = jnp.mean(x * x, axis=-1, keepdims=True)
    xn = ((x * lax.rsqrt(ms + EPS)) * g_ref[...]).astype(BF16)

    def proj(c0, width):
        return _dot(xn, w_ref[:, c0:c0 + width])

    gq_ref[...] = proj(_C_GQ, GLA_QK_W) * (GLA_DK ** -0.5)
    gk_ref[...] = proj(_C_GK, GLA_QK_W)
    gv_ref[...] = proj(_C_GV, GLA_V_W)
    gr_ref[...] = proj(_C_GR, GLA_V_W)

    ga_hi, ga_lo = _split_bf16(proj(_C_GA, LANES))
    wa_hi, wa_lo = _split_bf16(wa_ref[...])
    u = _dot(ga_hi, wa_hi) + _dot(ga_lo, wa_hi) + _dot(ga_hi, wa_lo) + ba_ref[...]
    la_ref[...] = (jnp.minimum(u, 0.0) - jnp.log(1.0 + jnp.exp(-jnp.abs(u)))) * (1.0 / GLA_TAU)

    q_s = _half_lane_rms(proj(_C_SQ, SB_W), qgain_ref[...])
    qs_ref[...] = (q_s * (SB_DIM ** -0.5 * LOG2E)).astype(BF16)
    k_s = _half_lane_rms(proj(_C_SK, SB_W), kgain_ref[...])
    ks_ref[...] = k_s
    ksb_ref[...] = k_s.astype(BF16)
    v_s = proj(_C_SV, SB_W)
    vs_ref[...] = v_s
    vsb_ref[...] = v_s.astype(BF16)


def _in_proj(h2, norm_g, w_in_r, wa_pad, ba, qgain, kgain):
    n, d = h2.shape
    tm = min(ROW_TILE, n)
    row = lambda w: pl.BlockSpec((tm, w), lambda i: (i, 0))
    full = lambda a: pl.BlockSpec(a.shape, lambda i: (0,) * a.ndim)
    outs = [(GLA_QK_W, F32), (GLA_QK_W, F32), (GLA_V_W, F32), (GLA_V_W, F32), (GLA_QK_W, F32),
            (SB_W, BF16), (SB_W, F32), (SB_W, F32), (SB_W, BF16), (SB_W, BF16)]
    return pl.pallas_call(
        _in_proj_kernel,
        grid=(n // tm,),
        in_specs=[row(d), full(norm_g), full(w_in_r), full(wa_pad), full(ba), full(qgain), full(kgain)],
        out_specs=[row(w) for w, _ in outs],
        out_shape=[jax.ShapeDtypeStruct((n, w), dt) for w, dt in outs],
        compiler_params=_cparams(("parallel",)),
        name="in_proj",
    )(h2, norm_g, w_in_r, wa_pad, ba, qgain, kgain)


def _gla_levels(c):
    levels = []
    l = c // 2
    while l >= 1:
        levels.append(l)
        l //= 2
    return levels


def _gla_constants(c):
    t = np.arange(c)
    tri = (t[None, :] <= t[:, None]).astype(np.float32)
    mats, masks = [tri], []
    for l in _gla_levels(c):
        mid = (t // (2 * l)) * (2 * l) + l
        mats.append((t[None, :] <= (mid[:, None] - 1)).astype(np.float32))
        same = (t[:, None] // (2 * l)) == (t[None, :] // (2 * l))
        masks.append((same & ((t[:, None] % (2 * l)) >= l) & ((t[None, :] % (2 * l)) < l)).astype(np.float32))
    masks.append(np.eye(c, dtype=np.float32))
    return np.concatenate(mats, axis=0), np.stack(masks)


def _gla_kernel(q_ref, k_ref, v_ref, la_ref, s0_ref, gmat_ref, mask_ref, o_ref, s_out_ref, st_ref,
                *, chunk, front_pad):
    c = pl.program_id(1)
    levels = _gla_levels(chunk)
    w = GLA_QK_W

    @pl.when(c == 0)
    def _():
        st_ref[...] = s0_ref[0].T

    q = q_ref[0]
    k = k_ref[0]
    la = la_ref[0]
    if front_pad:
        row = lax.broadcasted_iota(jnp.int32, la.shape, 0) + c * chunk
        la = jnp.where(row < front_pad, 0.0, la)
    la_hi, la_lo = _split_bf16(la)
    p = _dot(gmat_ref[...], jnp.concatenate([la_hi, la_lo], axis=1))
    p = p[:, :w] + p[:, w:]
    b = p[:chunk]
    lane = _lane_iota((1, w))
    head_masks = [(lane >= h * GLA_DK) & (lane < (h + 1) * GLA_DK) for h in range(GLA_HEADS)]

    scores = [jnp.zeros((chunk, chunk), F32) for _ in range(GLA_HEADS)]
    for i, _l in enumerate(levels):
        r = p[(i + 1) * chunk:(i + 2) * chunk]
        qt = q * jnp.exp(jnp.minimum(b - r, 0.0))
        kt = (k * jnp.exp(jnp.minimum(r - b, 0.0))).astype(BF16)
        for h in range(GLA_HEADS):
            qh = jnp.where(head_masks[h], qt, 0.0).astype(BF16)
            scores[h] = scores[h] + mask_ref[i] * _dot_nt(qh, kt)
    kb = k.astype(BF16)
    for h in range(GLA_HEADS):
        qh = jnp.where(head_masks[h], q, 0.0).astype(BF16)
        scores[h] = scores[h] + mask_ref[len(levels)] * _dot_nt(qh, kb)

    st = st_ref[...]
    st_b = st.astype(BF16)
    b_last = b[chunk - 1:chunk]
    q_in = q * jnp.exp(b)
    k_out = k * jnp.exp(b_last - b)
    upd = jnp.zeros_like(st)
    for h in range(GLA_HEADS):
        vh = v_ref[0, :, h * GLA_DV:(h + 1) * GLA_DV]
        vhb = vh.astype(BF16)
        qh = jnp.where(head_masks[h], q_in, 0.0).astype(BF16)
        o_ref[0, :, h * GLA_DV:(h + 1) * GLA_DV] = _dot(scores[h].astype(BF16), vhb) + _dot_nt(qh, st_b)
        kh = jnp.where(head_masks[h], k_out, 0.0).astype(BF16)
        upd = upd + _dot(vh.T.astype(BF16), kh)
    st_new = st * jnp.exp(b_last) + upd
    st_ref[...] = st_new

    @pl.when(c == pl.num_programs(1) - 1)
    def _():
        s_out_ref[0] = st_new.T


def _gla(q, k, v, la, s0, front_pad):
    bsz, t, _ = q.shape
    chunk = GLA_CHUNK
    gmat, masks = _gla_constants(chunk)
    gmat = jnp.asarray(gmat, BF16)
    masks = jnp.asarray(masks, F32)
    tok = lambda w: pl.BlockSpec((1, chunk, w), lambda b, c: (b, c, 0))
    per_b = pl.BlockSpec((1, GLA_QK_W, GLA_DV), lambda b, c: (b, 0, 0))
    return pl.pallas_call(
        functools.partial(_gla_kernel, chunk=chunk, front_pad=front_pad),
        grid=(bsz, t // chunk),
        in_specs=[tok(GLA_QK_W), tok(GLA_QK_W), tok(GLA_V_W), tok(GLA_QK_W), per_b,
                  pl.BlockSpec(gmat.shape, lambda b, c: (0, 0)),
                  pl.BlockSpec(masks.shape, lambda b, c: (0, 0, 0))],
        out_specs=[tok(GLA_V_W), per_b],
        out_shape=[jax.ShapeDtypeStruct((bsz, t, GLA_V_W), F32),
                   jax.ShapeDtypeStruct((bsz, GLA_QK_W, GLA_DV), F32)],
        scratch_shapes=[pltpu.VMEM((GLA_DV, GLA_QK_W), F32)],
        compiler_params=_cparams(("parallel", "arbitrary")),
        name="gla",
    )(q, k, v, la, s0, gmat, masks)


def _softplus2(z):
    neg_abs = lax.bitcast_convert_type(lax.bitcast_convert_type(z, jnp.uint32) | jnp.uint32(0x80000000), F32)
    return jnp.maximum(z, 0.0) + jnp.log2(1.0 + jnp.exp2(neg_abs))


def _sb_prompt_kernel(bias_ref, q_ref, k_ref, v_ref, ntri_ref, o_ref, acc_ref, c0_ref, c1_ref, *, tq, tk):
    hp = pl.program_id(1)
    i = pl.program_id(2)
    ndiag = tq // tk
    q = q_ref[0].astype(F32)
    lane_q = _lane_iota(q.shape)
    q0 = jnp.where(lane_q < SB_DIM, q, jnp.where(lane_q < SB_DIM + 2, 1.0, 0.0)).astype(BF16)
    q1 = jnp.where(lane_q >= SB_DIM, q, jnp.where(lane_q < 2, 1.0, 0.0)).astype(BF16)
    ntri = ntri_ref[...]
    lane_k = _lane_iota((tk, LANES))
    low_k = lane_k < SB_DIM

    def bias_lanes(h, first_lane):
        hi = jnp.full((tk, LANES), bias_ref[4 * hp + 2 * h], F32)
        lo = jnp.full((tk, LANES), bias_ref[4 * hp + 2 * h + 1], F32)
        return jnp.where(lane_k == first_lane, hi, jnp.where(lane_k == first_lane + 1, lo, 0.0)).astype(BF16)

    kbias0 = bias_lanes(0, SB_DIM)
    kbias1 = bias_lanes(1, 0)

    def head(qh, kh, vh, carry, mask):
        z = _dot_nt(qh, kh)
        sp = _softplus2(z)
        if mask is not None:
            sp = jnp.where(mask, sp, 0.0)
        between = _dot(sp.astype(BF16), ntri)
        a = jnp.exp2((z - sp) + between + jnp.concatenate([carry] * (tk // LANES), axis=1))
        if mask is not None:
            a = jnp.where(mask, a, 0.0)
        rowsum = jnp.broadcast_to(jnp.sum(sp, axis=-1, keepdims=True), carry.shape)
        return _dot(a.astype(BF16), vh), carry - rowsum

    def tile(kt, r0, mask, c0, c1):
        rows = pl.ds(pl.multiple_of(kt * tk, tk), tk)
        kb = k_ref[0, rows, :]
        vb = v_ref[0, rows, :]
        zk = jnp.zeros_like(vb)
        o0, c0 = head(q0[r0:], jnp.where(low_k, kb, kbias0), jnp.where(low_k, vb, zk), c0, mask)
        o1, c1 = head(q1[r0:], jnp.where(low_k, kbias1, kb), jnp.where(low_k, zk, vb), c1, mask)
        return o0 + o1, c0, c1

    nfull = i * ndiag

    acc = jnp.zeros((tq, LANES), F32)
    c0 = jnp.zeros((tq, LANES), F32)
    c1 = jnp.zeros((tq, LANES), F32)
    for d in reversed(range(ndiag)):
        r0 = d * tk
        nr = tq - r0
        mask = lax.broadcasted_iota(jnp.int32, (nr, tk), 0) > lax.broadcasted_iota(jnp.int32, (nr, tk), 1)
        o, c0n, c1n = tile(nfull + d, r0, mask, c0[r0:], c1[r0:])
        tail_rows = lambda full, new: new if r0 == 0 else jnp.concatenate([full[:r0], new], axis=0)
        acc = tail_rows(acc, acc[r0:] + o)
        c0 = tail_rows(c0, c0n)
        c1 = tail_rows(c1, c1n)
    acc_ref[...] = acc
    c0_ref[...] = c0
    c1_ref[...] = c1

    def body(it, carry):
        c0 = c0_ref[...]
        c1 = c1_ref[...]
        total = None
        for u in range(ndiag):
            o, c0, c1 = tile(nfull - 1 - (it * ndiag + u), 0, None, c0, c1)
            total = o if total is None else total + o
        acc_ref[...] += total
        c0_ref[...] = c0
        c1_ref[...] = c1
        return carry

    lax.fori_loop(0, i, body, 0)
    o_ref[0] = acc_ref[...]


def _sb_prompt(q, k, v, bias):
    bsz, t, _ = q.shape
    tq, tk = SB_TQ, SB_TK
    t_idx = np.arange(tk)
    ntri = jnp.asarray(-(t_idx[:, None] > t_idx[None, :]).astype(np.float32), BF16)
    grid_spec = pltpu.PrefetchScalarGridSpec(
        num_scalar_prefetch=1,
        grid=(bsz, SB_HEADS // 2, t // tq),
        in_specs=[pl.BlockSpec((1, tq, LANES), lambda b, hp, i, bias: (b, i, hp)),
                  pl.BlockSpec((1, t, LANES), lambda b, hp, i, bias: (b, 0, hp)),
                  pl.BlockSpec((1, t, LANES), lambda b, hp, i, bias: (b, 0, hp)),
                  pl.BlockSpec((tk, tk), lambda b, hp, i, bias: (0, 0))],
        out_specs=pl.BlockSpec((1, tq, LANES), lambda b, hp, i, bias: (b, i, hp)),
        scratch_shapes=[pltpu.VMEM((tq, LANES), F32)] * 3,
    )
    return pl.pallas_call(
        functools.partial(_sb_prompt_kernel, tq=tq, tk=tk),
        grid_spec=grid_spec,
        out_shape=jax.ShapeDtypeStruct((bsz, t, SB_W), F32),
        compiler_params=_cparams(("parallel", "parallel", "arbitrary")),
        name="sb_prompt",
    )(bias, q, k, v, ntri)


def _sb_sample_kernel(pt_ref, q_ref, bias_ref, kn_ref, vn_ref, *rest, n_new, page, pages_per_step):
    kc_refs = rest[:pages_per_step]
    vc_refs = rest[pages_per_step:2 * pages_per_step]
    ntri_ref, o_ref, acc_ref, car_ref = rest[2 * pages_per_step:]
    j = pl.program_id(1)
    rows = SB_HEADS * SUBLANES
    q = q_ref[0]
    bias = bias_ref[...]
    ntri = ntri_ref[...]

    def run(k_refs, v_refs, mask):
        n = len(k_refs)
        kt = jnp.concatenate([r[0].reshape(SB_W, page).astype(BF16) for r in k_refs], axis=1)
        vt = jnp.concatenate([r[0].reshape(SB_W, page).astype(BF16) for r in v_refs], axis=1)
        z_all = _dot(q, kt)
        zs, sps = [], []
        for r in range(n):
            z = z_all[:, r * page:(r + 1) * page] + bias
            sp = _softplus2(z)
            if mask is not None:
                sp = jnp.where(mask, sp, 0.0)
            zs.append(z)
            sps.append(sp)
        between = _dot(jnp.concatenate(sps, axis=0).astype(BF16), ntri)
        car = car_ref[...]
        a_list = []
        for r in range(n):
            a = jnp.exp2((zs[r] - sps[r]) + between[r * rows:(r + 1) * rows] + car)
            if mask is not None:
                a = jnp.where(mask, a, 0.0)
            a_list.append(a.astype(BF16))
            car = car - jnp.broadcast_to(jnp.sum(sps[r], axis=-1, keepdims=True), car.shape)
        acc_ref[...] += _dot_nt(jnp.concatenate(a_list, axis=1), vt)
        car_ref[...] = car

    @pl.when(j == 0)
    def _():
        acc_ref[...] = jnp.zeros_like(acc_ref)
        car_ref[...] = jnp.zeros_like(car_ref)
        t_i = lax.broadcasted_iota(jnp.int32, (rows, page), 0) % SUBLANES
        s_i = lax.broadcasted_iota(jnp.int32, (rows, page), 1)
        run([kn_ref], [vn_ref], (s_i < t_i) & (s_i < n_new))

    run(kc_refs, vc_refs, None)

    @pl.when(j == pl.num_programs(1) - 1)
    def _():
        acc = acc_ref[...]
        for h in range(SB_HEADS):
            o_ref[0, h] = acc[h * SUBLANES:(h + 1) * SUBLANES, h * SB_DIM:(h + 1) * SB_DIM]


def _sb_sample(q_bd, bias_rows, kt_new, vt_new, cache_kt, cache_vt, page_table, n_new):
    bsz = q_bd.shape[0]
    page = cache_kt.shape[3]
    n_pages = page_table.shape[1]
    pps = max(p for p in range(1, SB_SAMPLE_PAGES + 1) if n_pages % p == 0)
    rows = SB_HEADS * SUBLANES
    t_idx = np.arange(page)
    ntri = jnp.asarray(-(t_idx[:, None] > t_idx[None, :]).astype(np.float32), BF16)

    def cache_map(r):
        return lambda b, j, pt: (pt[b * n_pages + (n_pages - 1 - (j * pps + r))], 0, 0, 0)

    per_b3 = lambda b, j, pt: (b, 0, 0)
    per_b4 = lambda b, j, pt: (b, 0, 0, 0)
    page_block = (1, SB_HEADS, SB_DIM, page)
    grid_spec = pltpu.PrefetchScalarGridSpec(
        num_scalar_prefetch=1,
        grid=(bsz, n_pages // pps),
        in_specs=[pl.BlockSpec((1, rows, SB_W), per_b3),
                  pl.BlockSpec((rows, page), lambda b, j, pt: (0, 0)),
                  pl.BlockSpec(page_block, per_b4),
                  pl.BlockSpec(page_block, per_b4)]
                 + [pl.BlockSpec(page_block, cache_map(r)) for r in range(pps)]
                 + [pl.BlockSpec(page_block, cache_map(r)) for r in range(pps)]
                 + [pl.BlockSpec((page, page), lambda b, j, pt: (0, 0))],
        out_specs=pl.BlockSpec((1, SB_HEADS, SUBLANES, SB_DIM), per_b4),
        scratch_shapes=[pltpu.VMEM((rows, SB_W), F32), pltpu.VMEM((rows, page), F32)],
    )
    return pl.pallas_call(
        functools.partial(_sb_sample_kernel, n_new=n_new, page=page, pages_per_step=pps),
        grid_spec=grid_spec,
        out_shape=jax.ShapeDtypeStruct((bsz, SB_HEADS, SUBLANES, SB_DIM), F32),
        compiler_params=_cparams(("parallel", "arbitrary")),
        name="sb_sample",
    )(page_table.reshape(-1), q_bd, bias_rows, kt_new, vt_new, *([cache_kt] * pps), *([cache_vt] * pps), ntri)


def _mix_out_kernel(h_ref, og_ref, gr_ref, os_ref, ggain_ref, sgain_ref, wo_ref, fgain_ref, rw_ref, rb_ref,
                    h1_ref, xn_ref, eidx_ref, gate_ref):
    og = og_ref[...]
    parts = []
    for hh in range(GLA_HEADS):
        x = og[:, hh * GLA_DV:(hh + 1) * GLA_DV]
        parts.append(x * lax.rsqrt(jnp.mean(x * x, axis=-1, keepdims=True) + EPS))
    gr = gr_ref[...]
    og_n = (jnp.concatenate(parts, axis=1) * ggain_ref[...]) * (gr * (1.0 / (1.0 + jnp.exp(-gr))))
    os_n = _half_lane_rms(os_ref[...], sgain_ref[...])
    h1 = h_ref[...] + (_dot(og_n.astype(BF16), wo_ref[:GLA_V_W, :]) + _dot(os_n.astype(BF16), wo_ref[GLA_V_W:, :]))
    h1_ref[...] = h1

    xn = (h1 * lax.rsqrt(jnp.mean(h1 * h1, axis=-1, keepdims=True) + EPS)) * fgain_ref[...]
    for j in range(xn.shape[1] // LANES):
        xn_ref[pl.ds(j, xn.shape[0], stride=SLAB), :] = xn[:, j * LANES:(j + 1) * LANES]

    x_hi, x_lo = _split_bf16(xn)
    rw = rw_ref[...]
    l2 = _dot(x_hi, rw) + _dot(x_lo, rw)
    logits = l2[:, :LANES] + l2[:, LANES:] + rb_ref[...]
    lane = _lane_iota(logits.shape).astype(F32)
    big = jnp.float32(4 * LANES)
    neg = jnp.float32(-jnp.inf)

    is_g = (lane >= N_EXPERTS) & (lane < N_EXPERTS + N_GROUPS)
    lg = jnp.where(is_g, logits, neg)
    mg = jnp.max(lg, axis=-1, keepdims=True)
    g_val = 1.0 / jnp.sum(jnp.exp(lg - mg), axis=-1, keepdims=True)
    g_idx = jnp.min(jnp.where(lg == mg, lane, big), axis=-1, keepdims=True) - N_EXPERTS

    in_g = (lane >= g_idx * EXPERTS_PER_GROUP) & (lane < (g_idx + 1) * EXPERTS_PER_GROUP)
    le = jnp.where(in_g, logits, neg)
    m1 = jnp.max(le, axis=-1, keepdims=True)
    se = jnp.sum(jnp.exp(le - m1), axis=-1, keepdims=True)
    i1 = jnp.min(jnp.where(le == m1, lane, big), axis=-1, keepdims=True)
    le2 = jnp.where(lane == i1, neg, le)
    m2 = jnp.max(le2, axis=-1, keepdims=True)
    i2 = jnp.min(jnp.where(le2 == m2, lane, big), axis=-1, keepdims=True)
    p1 = 1.0 / se
    p2 = jnp.exp(m2 - m1) / se
    tot = p1 + p2
    w1 = g_val * p1 / tot
    w2 = g_val * p2 / tot
    eidx_ref[...] = jnp.where(lane == 0, i1, jnp.where(lane == 1, i2, 0.0))[:, :SUBLANES].astype(jnp.int32)
    gate_ref[...] = jnp.where(lane == 0, w1, jnp.where(lane == 1, w2, 0.0))[:, :SUBLANES]


def _mix_out(h2, og, gr, osb, ggain, sgain, w_out_b, fgain, rw, rb):
    n, d = h2.shape
    tm = min(ROW_TILE, n)
    row = lambda w: pl.BlockSpec((tm, w), lambda i: (i, 0))
    full = lambda a: pl.BlockSpec(a.shape, lambda i: (0,) * a.ndim)
    return pl.pallas_call(
        _mix_out_kernel,
        grid=(n // tm,),
        in_specs=[row(d), row(GLA_V_W), row(GLA_V_W), row(SB_W), full(ggain), full(sgain), full(w_out_b),
                  full(fgain), full(rw), full(rb)],
        out_specs=[row(d), pl.BlockSpec((tm * SLAB, LANES), lambda i: (i, 0)), row(SUBLANES),
                   row(SUBLANES)],
        out_shape=[jax.ShapeDtypeStruct((n, d), F32), jax.ShapeDtypeStruct((n * SLAB, LANES), F32),
                   jax.ShapeDtypeStruct((n, SUBLANES), jnp.int32), jax.ShapeDtypeStruct((n, SUBLANES), F32)],
        compiler_params=_cparams(("parallel",)),
        name="mix_out",
    )(h2, og, gr, osb, ggain, sgain, w_out_b, fgain, rw, rb)


def _moe_slots_kernel(eidx_ref, ltri_ref, dest_ref, cnt_ref, counts_ref, run_ref, pstart_ref):
    ph = pl.program_id(0)
    i = pl.program_id(1)
    e = eidx_ref[...]
    tm = e.shape[0]
    lane = _lane_iota((tm, LANES))
    oh0 = (lane == e[:, 0:1]).astype(F32)
    oh1 = (lane == e[:, 1:2]).astype(F32)
    tot0 = jnp.sum(oh0, axis=0, keepdims=True)
    tot1 = jnp.sum(oh1, axis=0, keepdims=True)

    @pl.when((ph == 0) & (i == 0))
    def _():
        counts_ref[...] = jnp.zeros_like(counts_ref)

    @pl.when(ph == 0)
    def _():
        counts_ref[...] += tot0 + tot1

    @pl.when((ph == 1) & (i == 0))
    def _():
        cnt = counts_ref[...]
        padded = jnp.floor((cnt + (MOE_BLOCK - 1)) * (1.0 / MOE_BLOCK)) * MOE_BLOCK
        x = jnp.broadcast_to(padded, (SUBLANES, LANES))
        l8 = _lane_iota((SUBLANES, LANES))
        s = 1
        while s < LANES:
            x = x + jnp.where(l8 >= s, pltpu.roll(x, s, axis=1), 0.0)
            s *= 2
        pstart_ref[...] = x[0:1] - padded
        run_ref[...] = jnp.zeros_like(run_ref)
        cnt_ref[...] = jnp.broadcast_to(cnt, (SUBLANES, LANES)).astype(jnp.int32)

    @pl.when(ph == 1)
    def _():
        ltri = ltri_ref[...]
        base0 = run_ref[...] + pstart_ref[...]
        c0 = _dot(ltri, oh0.astype(BF16))
        c1 = _dot(ltri, oh1.astype(BF16))
        d0 = jnp.sum(oh0 * (base0 + c0), axis=-1, keepdims=True)
        d1 = jnp.sum(oh1 * (base0 + tot0 + c1), axis=-1, keepdims=True)
        l8 = _lane_iota((tm, LANES))
        dest_ref[...] = jnp.where(l8 == 0, d0, jnp.where(l8 == 1, d1, 0.0))[:, :SUBLANES].astype(jnp.int32)
        run_ref[...] += tot0 + tot1


def _moe_slots(eidx):
    n = eidx.shape[0]
    tm = MOE_TILE
    t_idx = np.arange(tm)
    ltri = jnp.asarray((t_idx[None, :] < t_idx[:, None]).astype(np.float32), BF16)
    return pl.pallas_call(
        _moe_slots_kernel,
        grid=(2, n // tm),
        in_specs=[pl.BlockSpec((tm, SUBLANES), lambda ph, i: (i, 0)),
                  pl.BlockSpec((tm, tm), lambda ph, i: (0, 0))],
        out_specs=[pl.BlockSpec((tm, SUBLANES), lambda ph, i: (i * ph, 0)),
                   pl.BlockSpec((SUBLANES, LANES), lambda ph, i: (0, 0))],
        out_shape=[jax.ShapeDtypeStruct((n, SUBLANES), jnp.int32),
                   jax.ShapeDtypeStruct((SUBLANES, LANES), jnp.int32)],
        scratch_shapes=[pltpu.VMEM((1, LANES), F32)] * 3,
        compiler_params=_cparams(("arbitrary", "arbitrary")),
        name="moe_slots",
    )(eidx, ltri)


def _load_slots(d0_hbm, d1_hbm, d0_s, d1_s, isem, tile, tm):
    base = pl.multiple_of(tile * tm, tm)
    c0 = pltpu.make_async_copy(d0_hbm.at[pl.ds(base, tm)], d0_s, isem.at[0])
    c1 = pltpu.make_async_copy(d1_hbm.at[pl.ds(base, tm)], d1_s, isem.at[1])
    c0.start()
    c1.start()
    c0.wait()
    c1.wait()


def _on_parity(i, fn):
    for s in (0, 1):
        pl.when(i % 2 == s)(functools.partial(fn, s))


def _moe_dispatch_kernel(d0_hbm, d1_hbm, x_hbm, xs_in, xs_hbm, d0_a, d1_a, d0_b, d1_b, isem, sem, *, tm):
    del xs_in
    i = pl.program_id(0)
    last = pl.num_programs(0) - 1
    idx = ((d0_a, d1_a), (d0_b, d1_b))

    def copies(tile, s, r):
        src = x_hbm.at[_slab(tile * tm + r)]
        return (pltpu.make_async_copy(src, xs_hbm.at[_slab(idx[s][0][r])], sem.at[s, 0]),
                pltpu.make_async_copy(src, xs_hbm.at[_slab(idx[s][1][r])], sem.at[s, 1]))

    def start(tile, s):
        def body(r, carry):
            a, b = copies(tile, s, r)
            a.start()
            b.start()
            return carry
        lax.fori_loop(0, tm, body, 0, unroll=DMA_LOOP_UNROLL)

    def drain(tile, s):
        def body(r, carry):
            a, b = copies(tile, s, r)
            a.wait()
            b.wait()
            return carry
        lax.fori_loop(0, tm, body, 0, unroll=DMA_LOOP_UNROLL)

    def run(s):
        _load_slots(d0_hbm, d1_hbm, idx[s][0], idx[s][1], isem, i, tm)
        start(i, s)

        @pl.when(i > 0)
        def _():
            drain(i - 1, 1 - s)

        @pl.when(i == last)
        def _():
            drain(i, s)

    _on_parity(i, run)


def _moe_dispatch(d0, d1, xn_slab, m_pad):
    n = xn_slab.shape[0] // SLAB
    tm = MOE_TILE
    xs0 = jnp.zeros((m_pad * SLAB, LANES), F32)
    return pl.pallas_call(
        functools.partial(_moe_dispatch_kernel, tm=tm),
        grid=(n // tm,),
        in_specs=[pl.BlockSpec(memory_space=pl.ANY)] * 4,
        out_specs=pl.BlockSpec(memory_space=pl.ANY),
        out_shape=jax.ShapeDtypeStruct((m_pad * SLAB, LANES), F32),
        scratch_shapes=[pltpu.SMEM((tm,), jnp.int32)] * 4
                       + [pltpu.SemaphoreType.DMA((2,)), pltpu.SemaphoreType.DMA((2, 2))],
        input_output_aliases={3: 0},
        compiler_params=_cparams(("arbitrary",)),
        name="moe_dispatch",
    )(d0, d1, xn_slab, xs0)


def _moe_experts_kernel(be_ref, nu_ref, xs_ref, wg_ref, wu_ref, wd_ref, y_ref, wgb_ref, wub_ref, wdb_ref):
    i = pl.program_id(0)
    rows = xs_ref.shape[0] // SLAB

    @pl.when((i == 0) | (be_ref[i] != be_ref[jnp.maximum(i - 1, 0)]))
    def _():
        wgb_ref[...] = wg_ref[0].astype(BF16)
        wub_ref[...] = wu_ref[0].astype(BF16)
        wdb_ref[...] = wd_ref[0].astype(BF16)

    @pl.when(i < nu_ref[0])
    def _():
        x = jnp.concatenate([xs_ref[pl.ds(j, rows, stride=SLAB), :] for j in range(SLAB)], axis=1).astype(BF16)
        g = _dot(x, wgb_ref[...])
        u = _dot(x, wub_ref[...])
        hdn = (g * (1.0 / (1.0 + jnp.exp(-g)))) * u
        y = _dot(hdn.astype(BF16), wdb_ref[...])
        for j in range(SLAB):
            y_ref[pl.ds(j, rows, stride=SLAB), :] = y[:, j * LANES:(j + 1) * LANES]

    @pl.when(i >= nu_ref[0])
    def _():
        y_ref[...] = jnp.zeros_like(y_ref)


def _moe_experts(block_e, n_used, xs, wg, wu, wd):
    m_pad = xs.shape[0] // SLAB
    nb = m_pad // MOE_BLOCK
    d, de = wg.shape[1], wg.shape[2]
    rows = pl.BlockSpec((MOE_BLOCK * SLAB, LANES), lambda i, be, nu: (i, 0))
    grid_spec = pltpu.PrefetchScalarGridSpec(
        num_scalar_prefetch=2,
        grid=(nb,),
        in_specs=[rows,
                  pl.BlockSpec((1, d, de), lambda i, be, nu: (be[i], 0, 0)),
                  pl.BlockSpec((1, d, de), lambda i, be, nu: (be[i], 0, 0)),
                  pl.BlockSpec((1, de, d), lambda i, be, nu: (be[i], 0, 0))],
        out_specs=rows,
        scratch_shapes=[pltpu.VMEM((d, de), BF16), pltpu.VMEM((d, de), BF16), pltpu.VMEM((de, d), BF16)],
    )
    return pl.pallas_call(
        _moe_experts_kernel,
        grid_spec=grid_spec,
        out_shape=jax.ShapeDtypeStruct((m_pad * SLAB, LANES), F32),
        compiler_params=_cparams(("arbitrary",)),
        name="moe_experts",
    )(block_e, n_used, xs, wg, wu, wd)


def _moe_combine_kernel(d0_hbm, d1_hbm, gate_ref, h1_ref, y_hbm, out_ref, d0_a, d1_a, d0_b, d1_b,
                        buf0_a, buf1_a, buf0_b, buf1_b, isem, sem, *, tm):
    i = pl.program_id(0)
    last = pl.num_programs(0) - 1
    idx = ((d0_a, d1_a), (d0_b, d1_b))
    bufs = ((buf0_a, buf1_a), (buf0_b, buf1_b))
    cr = MOE_COMBINE_ROWS

    def copies(s, r):
        return (pltpu.make_async_copy(y_hbm.at[_slab(idx[s][0][r])], bufs[s][0].at[_slab(r)], sem.at[s, 0]),
                pltpu.make_async_copy(y_hbm.at[_slab(idx[s][1][r])], bufs[s][1].at[_slab(r)], sem.at[s, 1]))

    def gather(tile, s):
        _load_slots(d0_hbm, d1_hbm, idx[s][0], idx[s][1], isem, tile, tm)

        def body(r, carry):
            a, b = copies(s, r)
            a.start()
            b.start()
            return carry
        lax.fori_loop(0, tm, body, 0, unroll=DMA_LOOP_UNROLL)

    def run(s):
        @pl.when(i == 0)
        def _():
            gather(i, s)

        @pl.when(i < last)
        def _():
            gather(i + 1, 1 - s)

        def drain(r, carry):
            a, b = copies(s, r)
            a.wait()
            b.wait()
            return carry
        lax.fori_loop(0, tm, drain, 0, unroll=DMA_LOOP_UNROLL)

        def chunk(c, carry):
            r0 = pl.multiple_of(c * cr, cr)
            rows = pl.ds(r0, cr)
            w0 = jnp.broadcast_to(gate_ref[rows, 0:1], (cr, LANES))
            w1 = jnp.broadcast_to(gate_ref[rows, 1:2], (cr, LANES))
            for j in range(SLAB):
                srows = pl.ds(r0 * SLAB + j, cr, stride=SLAB)
                y = bufs[s][0][srows, :] * w0 + bufs[s][1][srows, :] * w1
                out_ref[rows, j * LANES:(j + 1) * LANES] = h1_ref[rows, j * LANES:(j + 1) * LANES] + y
            return carry
        lax.fori_loop(0, tm // cr, chunk, 0)

    _on_parity(i, run)


def _moe_combine(d0, d1, gate, h1, y_rows):
    n, d = h1.shape
    tm = MOE_TILE
    return pl.pallas_call(
        functools.partial(_moe_combine_kernel, tm=tm),
        grid=(n // tm,),
        in_specs=[pl.BlockSpec(memory_space=pl.ANY), pl.BlockSpec(memory_space=pl.ANY),
                  pl.BlockSpec((tm, SUBLANES), lambda i: (i, 0)),
                  pl.BlockSpec((tm, d), lambda i: (i, 0)),
                  pl.BlockSpec(memory_space=pl.ANY)],
        out_specs=pl.BlockSpec((tm, d), lambda i: (i, 0)),
        out_shape=jax.ShapeDtypeStruct((n, d), F32),
        scratch_shapes=[pltpu.SMEM((tm,), jnp.int32)] * 4 + [pltpu.VMEM((tm * SLAB, LANES), F32)] * 4
                       + [pltpu.SemaphoreType.DMA((2,)), pltpu.SemaphoreType.DMA((2, 2))],
        compiler_params=_cparams(("arbitrary",)),
        name="moe_combine",
    )(d0, d1, gate, h1, y_rows)


def _moe(h1, xn_slab, eidx, gate, wg, wu, wd):
    n = h1.shape[0]
    assert h1.shape[1] == SLAB * LANES
    n_pad = -(-n // MOE_TILE) * MOE_TILE
    if n_pad != n:
        padr = lambda a, k: jnp.pad(a, ((0, k * (n_pad - n)),) + ((0, 0),) * (a.ndim - 1))
        h1, xn_slab, eidx, gate = padr(h1, 1), padr(xn_slab, SLAB), padr(eidx, 1), padr(gate, 1)
    dest, cnt = _moe_slots(eidx)
    d0, d1 = dest[:, 0], dest[:, 1]
    nb = (2 * n_pad) // MOE_BLOCK + N_EXPERTS
    counts = cnt[0, :N_EXPERTS]
    pend = jnp.cumsum((counts + MOE_BLOCK - 1) // MOE_BLOCK)
    block_e = jnp.minimum(jnp.sum(pend[None, :] <= jnp.arange(nb, dtype=jnp.int32)[:, None], axis=1),
                          N_EXPERTS - 1).astype(jnp.int32)
    n_used = pend[-1:].astype(jnp.int32)
    xs = _moe_dispatch(d0, d1, xn_slab, nb * MOE_BLOCK)
    y_rows = _moe_experts(block_e, n_used, xs, wg, wu, wd)
    return _moe_combine(d0, d1, gate, h1, y_rows)[:n]


def kernel(x_prompt, x_sample, cache_sb_k, cache_sb_v, state_gla, page_table, meta_tokens, norm_mix_gain, w_in, gla_w_alpha, gla_b_alpha, gla_out_gain, sb_q_gain, sb_k_gain, sb_logit_bias, sb_out_gain, w_out, norm_ffn_gain, router_group, router_group_b, router_expert, router_expert_b, w_gate, w_up, w_down):
    bsz, seq, d = x_prompt.shape
    dbs, dseq, _ = x_sample.shape
    depth = w_in.shape[0]
    page = cache_sb_k.shape[2]
    t_real = seq + N_META
    t_pad = -(-t_real // SB_TQ) * SB_TQ
    fpad = t_pad - t_real

    hp = jnp.concatenate([jnp.zeros((bsz, fpad, d), x_prompt.dtype),
                          jnp.broadcast_to(meta_tokens[None].astype(x_prompt.dtype), (bsz, N_META, d)),
                          x_prompt], axis=1).reshape(bsz * t_pad, d)
    hs = x_sample.reshape(dbs * dseq, d)

    outs = {k: [] for k in ("kp", "vp", "sp", "ks", "vs", "ss")}
    for l in range(depth):
        w = w_in[l]
        w_in_r = jnp.concatenate([w[:, :1536], w[:, 1552:3088], w[:, 1536:1552],
                                  jnp.zeros((d, _W_IN_COLS - 3088), w.dtype)], axis=1).astype(BF16)
        wa_pad = jnp.pad(gla_w_alpha[l], ((0, LANES - GLA_RANK), (0, 0)))
        ba = gla_b_alpha[l][None]
        norm_g = norm_mix_gain[l][None]
        qgain = jnp.tile(sb_q_gain[l], SB_HEADS)[None]
        kgain = jnp.tile(sb_k_gain[l], SB_HEADS)[None]
        ggain = jnp.tile(gla_out_gain[l], GLA_HEADS)[None]
        sgain = jnp.tile(sb_out_gain[l], SB_HEADS)[None]
        fgain = norm_ffn_gain[l][None]
        w_out_b = w_out[l].astype(BF16)
        r_all = jnp.concatenate([router_expert[l].transpose(1, 0, 2).reshape(d, N_EXPERTS), router_group[l],
                                 jnp.zeros((d, LANES - N_EXPERTS - N_GROUPS), F32)], axis=1)
        r_hi = r_all.astype(BF16)
        r_lo = (r_all - r_hi.astype(F32)).astype(BF16)
        rw = jnp.concatenate([r_hi, r_lo], axis=1)
        rb = jnp.concatenate([router_expert_b[l].reshape(N_EXPERTS), router_group_b[l],
                              jnp.zeros((LANES - N_EXPERTS - N_GROUPS,), F32)])[None]
        wg, wu, wd = w_gate[l], w_up[l], w_down[l]
        bias2 = sb_logit_bias[l].astype(F32) * LOG2E
        b_hi = bias2.astype(BF16).astype(F32)
        b_lo = (bias2 - b_hi).astype(BF16).astype(F32)
        bias_hl = jnp.stack([b_hi, b_lo], axis=1).reshape(-1)

        gq, gk, gv, gr, la, qs, ks, vs, ksb, vsb = _in_proj(hp, norm_g, w_in_r, wa_pad, ba, qgain, kgain)
        b3 = lambda a: a.reshape(bsz, t_pad, a.shape[-1])
        o_g, s_p = _gla(b3(gq), b3(gk), b3(gv), b3(la), jnp.zeros((bsz, GLA_QK_W, GLA_DV), F32), fpad)
        o_s = _sb_prompt(b3(qs), b3(ksb), b3(vsb), bias_hl)
        h1, xn3, eidx, gate = _mix_out(hp, o_g.reshape(-1, GLA_V_W), gr, o_s.reshape(-1, SB_W), ggain, sgain,
                                       w_out_b, fgain, rw, rb)
        hp = _moe(h1, xn3, eidx, gate, wg, wu, wd)
        outs["kp"].append(b3(ks)[:, fpad:].reshape(bsz, t_real, SB_HEADS, SB_DIM))
        outs["vp"].append(b3(vs)[:, fpad:].reshape(bsz, t_real, SB_HEADS, SB_DIM))
        outs["sp"].append(s_p.reshape(bsz, GLA_HEADS, GLA_DK, GLA_DV))

        gq, gk, gv, gr, la, qs, ks, vs, ksb, vsb = _in_proj(hs, norm_g, w_in_r, wa_pad, ba, qgain, kgain)
        cpad = GLA_CHUNK - dseq
        c3 = lambda a: jnp.pad(a.reshape(dbs, dseq, a.shape[-1]), ((0, 0), (cpad, 0), (0, 0)))
        o_g, s_s = _gla(c3(gq), c3(gk), c3(gv), c3(la), state_gla[l].reshape(dbs, GLA_QK_W, GLA_DV), cpad)
        o_g = o_g[:, cpad:].reshape(dbs * dseq, GLA_V_W)
        q4 = qs.reshape(dbs, dseq, SB_HEADS, SB_DIM).transpose(0, 2, 1, 3)
        q4 = jnp.pad(q4, ((0, 0), (0, 0), (0, SUBLANES - dseq), (0, 0)))
        eye = jnp.eye(SB_HEADS, dtype=q4.dtype)
        q_bd = (q4[:, :, :, None, :] * eye[None, :, None, :, None]).reshape(dbs, SB_HEADS * SUBLANES, SB_W)
        bias_rows = jnp.broadcast_to(jnp.repeat(bias2, SUBLANES)[:, None], (SB_HEADS * SUBLANES, page))
        to_t = lambda a: jnp.pad(a.reshape(dbs, dseq, SB_HEADS, SB_DIM).transpose(0, 2, 3, 1),
                                 ((0, 0), (0, 0), (0, 0), (0, page - dseq)))
        cache_kt = cache_sb_k[l].transpose(0, 2, 3, 1)
        cache_vt = cache_sb_v[l].transpose(0, 2, 3, 1)
        o4 = _sb_sample(q_bd, bias_rows, to_t(ks), to_t(vs), cache_kt, cache_vt, page_table, dseq)
        o_s = o4[:, :, :dseq].transpose(0, 2, 1, 3).reshape(dbs * dseq, SB_W)
        h1, xn3, eidx, gate = _mix_out(hs, o_g, gr, o_s, ggain, sgain, w_out_b, fgain, rw, rb)
        hs = _moe(h1, xn3, eidx, gate, wg, wu, wd)
        outs["ks"].append(ks.reshape(dbs, dseq, SB_HEADS, SB_DIM))
        outs["vs"].append(vs.reshape(dbs, dseq, SB_HEADS, SB_DIM))
        outs["ss"].append(s_s.reshape(dbs, GLA_HEADS, GLA_DK, GLA_DV))

    y_prompt = hp.reshape(bsz, t_pad, d)[:, fpad + N_META:]
    y_sample = hs.reshape(dbs, dseq, d)
    return (y_prompt, y_sample, jnp.stack(outs["kp"]), jnp.stack(outs["vp"]), jnp.stack(outs["sp"]),
            jnp.stack(outs["ks"]), jnp.stack(outs["vs"]), jnp.stack(outs["ss"]))
```

```python
import functools

import jax
import jax.numpy as jnp
import numpy as np
from jax import lax
from jax.experimental import pallas as pl
from jax.experimental.pallas import tpu as pltpu

F32 = jnp.float32
BF16 = jnp.bfloat16

N_META = 16
GLA_HEADS = 4
GLA_DK = 64
GLA_DV = 128
GLA_RANK = 16
GLA_TAU = 16.0
GLA_QK_W = GLA_HEADS * GLA_DK
GLA_V_W = GLA_HEADS * GLA_DV
SB_HEADS = 8
SB_DIM = 64
SB_W = SB_HEADS * SB_DIM
N_GROUPS = 4
EXPERTS_PER_GROUP = 8
N_EXPERTS = N_GROUPS * EXPERTS_PER_GROUP
EPS = 1e-6
LOG2E = 1.4426950408889634

LANES = 128
SUBLANES = 8
VMEM_LIMIT_BYTES = 56 * 1024 * 1024

ROW_TILE = 512
GLA_CHUNK = 128
GLA_BATCH = 2
SB_TQ = 768
SB_TK = 256
SB_TRIP_BLOCKS = 2
SB_SAMPLE_PAGES = 8
MOE_TILE = 1024
MOE_BLOCK = 256
MOE_COMBINE_ROWS = 128
DMA_LOOP_UNROLL = 8
SLAB = 8

_C_GQ, _C_GK, _C_GV, _C_GR, _C_SQ, _C_SK, _C_SV, _C_GA = 0, 256, 512, 1024, 1536, 2048, 2560, 3072
_W_IN_COLS = 3200


def _cparams(sem):
    return pltpu.CompilerParams(dimension_semantics=sem, vmem_limit_bytes=VMEM_LIMIT_BYTES)


def _dot(a, b):
    return jnp.dot(a, b, preferred_element_type=F32)


def _dot_nt(a, b):
    return lax.dot_general(a, b, (((1,), (1,)), ((), ())), preferred_element_type=F32)


def _split_bf16(x):
    hi = x.astype(BF16)
    lo = (x - hi.astype(F32)).astype(BF16)
    return hi, lo


def _slab(r):
    return pl.ds(pl.multiple_of(r * SLAB, SLAB), SLAB)


def _lane_iota(shape):
    return lax.broadcasted_iota(jnp.int32, shape, len(shape) - 1)


def _half_lane_rms(x, gain):
    outs = []
    for g in range(x.shape[1] // LANES):
        xg = x[:, g * LANES:(g + 1) * LANES]
        x2 = xg * xg
        low = _lane_iota(xg.shape) < SB_DIM
        s_all = jnp.sum(x2, axis=-1, keepdims=True)
        s_lo = jnp.sum(jnp.where(low, x2, 0.0), axis=-1, keepdims=True)
        ms = jnp.where(low, s_lo, s_all - s_lo) * (1.0 / SB_DIM)
        outs.append(xg * lax.rsqrt(ms + EPS))
    return jnp.concatenate(outs, axis=1) * gain


def _in_proj_kernel(x_ref, g_ref, w_ref, wa_ref, ba_ref, qgain_ref, kgain_ref,
                    gq_ref, gk_ref, gv_ref, gr_ref, la_ref, qs_ref, ks_ref, vs_ref, ksb_ref, vsb_ref):
    x = x_ref[...]
    ms = jnp.mean(x * x, axis=-1, keepdims=True)
    xn = ((x * lax.rsqrt(ms + EPS)) * g_ref[...]).astype(BF16)

    def proj(c0, width):
        return _dot(xn, w_ref[:, c0:c0 + width])

    gq_ref[...] = proj(_C_GQ, GLA_QK_W) * (GLA_DK ** -0.5)
    gk_ref[...] = proj(_C_GK, GLA_QK_W)
    gv_ref[...] = proj(_C_GV, GLA_V_W)
    gr_ref[...] = proj(_C_GR, GLA_V_W)

    ga_hi, ga_lo = _split_bf16(proj(_C_GA, LANES))
    wa_hi, wa_lo = _split_bf16(wa_ref[...])
    u = _dot(ga_hi, wa_hi) + _dot(ga_lo, wa_hi) + _dot(ga_hi, wa_lo) + ba_ref[...]
    la_ref[...] = (jnp.minimum(u, 0.0) - jnp.log(1.0 + jnp.exp(-jnp.abs(u)))) * (1.0 / GLA_TAU)

    q_s = _half_lane_rms(proj(_C_SQ, SB_W), qgain_ref[...])
    qs_ref[...] = (q_s * (SB_DIM ** -0.5 * LOG2E)).astype(BF16)
    k_s = _half_lane_rms(proj(_C_SK, SB_W), kgain_ref[...])
    ks_ref[...] = k_s
    ksb_ref[...] = k_s.astype(BF16)
    v_s = proj(_C_SV, SB_W)
    vs_ref[...] = v_s
    vsb_ref[...] = v_s.astype(BF16)


def _in_proj(h2, norm_g, w_in_r, wa_pad, ba, qgain, kgain):
    n, d = h2.shape
    tm = min(ROW_TILE, n)
    row = lambda w: pl.BlockSpec((tm, w), lambda i: (i, 0))
    full = lambda a: pl.BlockSpec(a.shape, lambda i: (0,) * a.ndim)
    outs = [(GLA_QK_W, F32), (GLA_QK_W, F32), (GLA_V_W, F32), (GLA_V_W, F32), (GLA_QK_W, F32),
            (SB_W, BF16), (SB_W, F32), (SB_W, F32), (SB_W, BF16), (SB_W, BF16)]
    return pl.pallas_call(
        _in_proj_kernel,
        grid=(n // tm,),
        in_specs=[row(d), full(norm_g), full(w_in_r), full(wa_pad), full(ba), full(qgain), full(kgain)],
        out_specs=[row(w) for w, _ in outs],
        out_shape=[jax.ShapeDtypeStruct((n, w), dt) for w, dt in outs],
        compiler_params=_cparams(("parallel",)),
        name="in_proj",
    )(h2, norm_g, w_in_r, wa_pad, ba, qgain, kgain)


def _gla_levels(c):
    levels = []
    l = c // 2
    while l >= 1:
        levels.append(l)
        l //= 2
    return levels


def _gla_constants(c):
    t = np.arange(c)
    tri = (t[None, :] <= t[:, None]).astype(np.float32)
    mats, masks = [tri], []
    for l in _gla_levels(c):
        mid = (t // (2 * l)) * (2 * l) + l
        mats.append((t[None, :] <= (mid[:, None] - 1)).astype(np.float32))
        same = (t[:, None] // (2 * l)) == (t[None, :] // (2 * l))
        masks.append((same & ((t[:, None] % (2 * l)) >= l) & ((t[None, :] % (2 * l)) < l)).astype(np.float32))
    masks.append(np.eye(c, dtype=np.float32))
    return np.concatenate(mats, axis=0), np.stack(masks)


def _gla_kernel(q_ref, k_ref, v_ref, la_ref, s0_ref, gmat_ref, mask_ref, o_ref, s_out_ref, st_ref,
                *, chunk, front_pad):
    for bb in range(q_ref.shape[0]):
        _gla_chunk(bb, q_ref, k_ref, v_ref, la_ref, s0_ref, gmat_ref, mask_ref, o_ref, s_out_ref, st_ref,
                   chunk, front_pad)


def _gla_chunk(bb, q_ref, k_ref, v_ref, la_ref, s0_ref, gmat_ref, mask_ref, o_ref, s_out_ref, st_ref,
               chunk, front_pad):
    c = pl.program_id(1)
    levels = _gla_levels(chunk)
    w = GLA_QK_W

    @pl.when(c == 0)
    def _():
        st_ref[bb] = s0_ref[bb].T

    q = q_ref[bb]
    k = k_ref[bb]
    la = la_ref[bb]
    if front_pad:
        row = lax.broadcasted_iota(jnp.int32, la.shape, 0) + c * chunk
        la = jnp.where(row < front_pad, 0.0, la)
    la_hi, la_lo = _split_bf16(la)
    p = _dot(gmat_ref[...], jnp.concatenate([la_hi, la_lo], axis=1))
    p = p[:, :w] + p[:, w:]
    b = p[:chunk]
    lane = _lane_iota((1, w))
    head_masks = [(lane >= h * GLA_DK) & (lane < (h + 1) * GLA_DK) for h in range(GLA_HEADS)]

    scores = [jnp.zeros((chunk, chunk), F32) for _ in range(GLA_HEADS)]
    for i, _l in enumerate(levels):
        r = p[(i + 1) * chunk:(i + 2) * chunk]
        qt = q * jnp.exp(jnp.minimum(b - r, 0.0))
        kt = (k * jnp.exp(jnp.minimum(r - b, 0.0))).astype(BF16)
        for h in range(GLA_HEADS):
            qh = jnp.where(head_masks[h], qt, 0.0).astype(BF16)
            scores[h] = scores[h] + mask_ref[i] * _dot_nt(qh, kt)
    kb = k.astype(BF16)
    for h in range(GLA_HEADS):
        qh = jnp.where(head_masks[h], q, 0.0).astype(BF16)
        scores[h] = scores[h] + mask_ref[len(levels)] * _dot_nt(qh, kb)

    st = st_ref[bb]
    st_b = st.astype(BF16)
    b_last = b[chunk - 1:chunk]
    q_in = q * jnp.exp(b)
    k_out = k * jnp.exp(b_last - b)
    upd = jnp.zeros_like(st)
    for h in range(GLA_HEADS):
        vh = v_ref[bb, :, h * GLA_DV:(h + 1) * GLA_DV]
        vhb = vh.astype(BF16)
        qh = jnp.where(head_masks[h], q_in, 0.0).astype(BF16)
        o_ref[bb, :, h * GLA_DV:(h + 1) * GLA_DV] = _dot(scores[h].astype(BF16), vhb) + _dot_nt(qh, st_b)
        kh = jnp.where(head_masks[h], k_out, 0.0).astype(BF16)
        upd = upd + _dot(vh.T.astype(BF16), kh)
    st_new = st * jnp.exp(b_last) + upd
    st_ref[bb] = st_new

    @pl.when(c == pl.num_programs(1) - 1)
    def _():
        s_out_ref[bb] = st_new.T


def _gla(q, k, v, la, s0, front_pad):
    bsz, t, _ = q.shape
    chunk = GLA_CHUNK
    gmat, masks = _gla_constants(chunk)
    gmat = jnp.asarray(gmat, BF16)
    masks = jnp.asarray(masks, F32)
    nb = GLA_BATCH
    assert bsz % nb == 0
    tok = lambda w: pl.BlockSpec((nb, chunk, w), lambda b, c: (b, c, 0))
    per_b = pl.BlockSpec((nb, GLA_QK_W, GLA_DV), lambda b, c: (b, 0, 0))
    return pl.pallas_call(
        functools.partial(_gla_kernel, chunk=chunk, front_pad=front_pad),
        grid=(bsz // nb, t // chunk),
        in_specs=[tok(GLA_QK_W), tok(GLA_QK_W), tok(GLA_V_W), tok(GLA_QK_W), per_b,
                  pl.BlockSpec(gmat.shape, lambda b, c: (0, 0)),
                  pl.BlockSpec(masks.shape, lambda b, c: (0, 0, 0))],
        out_specs=[tok(GLA_V_W), per_b],
        out_shape=[jax.ShapeDtypeStruct((bsz, t, GLA_V_W), F32),
                   jax.ShapeDtypeStruct((bsz, GLA_QK_W, GLA_DV), F32)],
        scratch_shapes=[pltpu.VMEM((nb, GLA_DV, GLA_QK_W), F32)],
        compiler_params=_cparams(("parallel", "arbitrary")),
        name="gla",
    )(q, k, v, la, s0, gmat, masks)


def _softplus2(z):
    neg_abs = lax.bitcast_convert_type(lax.bitcast_convert_type(z, jnp.uint32) | jnp.uint32(0x80000000), F32)
    return jnp.maximum(z, 0.0) + jnp.log2(1.0 + jnp.exp2(neg_abs))


def _sb_prompt_kernel(bias_ref, q_ref, k_ref, v_ref, ntri_ref, o_ref, acc_ref, c0_ref, c1_ref, *, tq, tk):
    hp = pl.program_id(1)
    i = pl.program_id(2)
    ndiag = tq // tk
    q = q_ref[0].astype(F32)
    lane_q = _lane_iota(q.shape)
    q0 = jnp.where(lane_q < SB_DIM, q, jnp.where(lane_q < SB_DIM + 2, 1.0, 0.0)).astype(BF16)
    q1 = jnp.where(lane_q >= SB_DIM, q, jnp.where(lane_q < 2, 1.0, 0.0)).astype(BF16)
    ntri = ntri_ref[...]
    lane_k = _lane_iota((tk, LANES))
    low_k = lane_k < SB_DIM

    def bias_lanes(h, first_lane):
        hi = jnp.full((tk, LANES), bias_ref[4 * hp + 2 * h], F32)
        lo = jnp.full((tk, LANES), bias_ref[4 * hp + 2 * h + 1], F32)
        return jnp.where(lane_k == first_lane, hi, jnp.where(lane_k == first_lane + 1, lo, 0.0)).astype(BF16)

    kbias0 = bias_lanes(0, SB_DIM)
    kbias1 = bias_lanes(1, 0)

    def head(qh, kh, vh, carry, mask):
        z = _dot_nt(qh, kh)
        sp = _softplus2(z)
        if mask is not None:
            sp = jnp.where(mask, sp, 0.0)
        between = _dot(sp.astype(BF16), ntri)
        a = jnp.exp2((z - sp) + between + jnp.concatenate([carry] * (tk // LANES), axis=1))
        if mask is not None:
            a = jnp.where(mask, a, 0.0)
        rowsum = jnp.broadcast_to(jnp.sum(sp, axis=-1, keepdims=True), carry.shape)
        return _dot(a.astype(BF16), vh), carry - rowsum

    def tile(kt, r0, mask, c0, c1):
        rows = pl.ds(pl.multiple_of(kt * tk, tk), tk)
        kb = k_ref[0, rows, :]
        vb = v_ref[0, rows, :]
        zk = jnp.zeros_like(vb)
        o0, c0 = head(q0[r0:], jnp.where(low_k, kb, kbias0), jnp.where(low_k, vb, zk), c0, mask)
        o1, c1 = head(q1[r0:], jnp.where(low_k, kbias1, kb), jnp.where(low_k, zk, vb), c1, mask)
        return o0 + o1, c0, c1

    nfull = i * ndiag

    acc = jnp.zeros((tq, LANES), F32)
    c0 = jnp.zeros((tq, LANES), F32)
    c1 = jnp.zeros((tq, LANES), F32)
    for d in reversed(range(ndiag)):
        r0 = d * tk
        nr = tq - r0
        mask = lax.broadcasted_iota(jnp.int32, (nr, tk), 0) > lax.broadcasted_iota(jnp.int32, (nr, tk), 1)
        o, c0n, c1n = tile(nfull + d, r0, mask, c0[r0:], c1[r0:])
        tail_rows = lambda full, new: new if r0 == 0 else jnp.concatenate([full[:r0], new], axis=0)
        acc = tail_rows(acc, acc[r0:] + o)
        c0 = tail_rows(c0, c0n)
        c1 = tail_rows(c1, c1n)
    acc_ref[...] = acc
    c0_ref[...] = c0
    c1_ref[...] = c1

    def trip(first_kt, ntiles):
        c0 = c0_ref[...]
        c1 = c1_ref[...]
        total = None
        for u in range(ntiles):
            o, c0, c1 = tile(first_kt - u, 0, None, c0, c1)
            total = o if total is None else total + o
        acc_ref[...] += total
        c0_ref[...] = c0
        c1_ref[...] = c1

    per_trip = SB_TRIP_BLOCKS * ndiag

    def body(it, carry):
        trip(nfull - 1 - it * per_trip, per_trip)
        return carry

    lax.fori_loop(0, i // SB_TRIP_BLOCKS, body, 0)
    for rem in range(1, SB_TRIP_BLOCKS):
        @pl.when(i % SB_TRIP_BLOCKS == rem)
        def _():
            trip(rem * ndiag - 1, rem * ndiag)
    o_ref[0] = acc_ref[...]


def _sb_prompt(q, k, v, bias):
    bsz, t, _ = q.shape
    tq, tk = SB_TQ, SB_TK
    t_idx = np.arange(tk)
    ntri = jnp.asarray(-(t_idx[:, None] > t_idx[None, :]).astype(np.float32), BF16)
    grid_spec = pltpu.PrefetchScalarGridSpec(
        num_scalar_prefetch=1,
        grid=(bsz, SB_HEADS // 2, t // tq),
        in_specs=[pl.BlockSpec((1, tq, LANES), lambda b, hp, i, bias: (b, i, hp)),
                  pl.BlockSpec((1, t, LANES), lambda b, hp, i, bias: (b, 0, hp)),
                  pl.BlockSpec((1, t, LANES), lambda b, hp, i, bias: (b, 0, hp)),
                  pl.BlockSpec((tk, tk), lambda b, hp, i, bias: (0, 0))],
        out_specs=pl.BlockSpec((1, tq, LANES), lambda b, hp, i, bias: (b, i, hp)),
        scratch_shapes=[pltpu.VMEM((tq, LANES), F32)] * 3,
    )
    return pl.pallas_call(
        functools.partial(_sb_prompt_kernel, tq=tq, tk=tk),
        grid_spec=grid_spec,
        out_shape=jax.ShapeDtypeStruct((bsz, t, SB_W), F32),
        compiler_params=_cparams(("parallel", "parallel", "arbitrary")),
        name="sb_prompt",
    )(bias, q, k, v, ntri)


def _sb_sample_kernel(pt_ref, q_ref, bias_ref, kn_ref, vn_ref, *rest, n_new, page, pages_per_step):
    kc_refs = rest[:pages_per_step]
    vc_refs = rest[pages_per_step:2 * pages_per_step]
    ntri_ref, o_ref, acc_ref, car_ref = rest[2 * pages_per_step:]
    j = pl.program_id(1)
    rows = SB_HEADS * SUBLANES
    q = q_ref[0]
    bias = bias_ref[...]
    ntri = ntri_ref[...]

    def run(k_refs, v_refs, mask):
        n = len(k_refs)
        kt = jnp.concatenate([r[0].reshape(SB_W, page).astype(BF16) for r in k_refs], axis=1)
        vt = jnp.concatenate([r[0].reshape(SB_W, page).astype(BF16) for r in v_refs], axis=1)
        z_all = _dot(q, kt)
        zs, sps = [], []
        for r in range(n):
            z = z_all[:, r * page:(r + 1) * page] + bias
            sp = _softplus2(z)
            if mask is not None:
                sp = jnp.where(mask, sp, 0.0)
            zs.append(z)
            sps.append(sp)
        between = _dot(jnp.concatenate(sps, axis=0).astype(BF16), ntri)
        car = car_ref[...]
        a_list = []
        for r in range(n):
            a = jnp.exp2((zs[r] - sps[r]) + between[r * rows:(r + 1) * rows] + car)
            if mask is not None:
                a = jnp.where(mask, a, 0.0)
            a_list.append(a.astype(BF16))
            car = car - jnp.broadcast_to(jnp.sum(sps[r], axis=-1, keepdims=True), car.shape)
        acc_ref[...] += _dot_nt(jnp.concatenate(a_list, axis=1), vt)
        car_ref[...] = car

    @pl.when(j == 0)
    def _():
        acc_ref[...] = jnp.zeros_like(acc_ref)
        car_ref[...] = jnp.zeros_like(car_ref)
        t_i = lax.broadcasted_iota(jnp.int32, (rows, page), 0) % SUBLANES
        s_i = lax.broadcasted_iota(jnp.int32, (rows, page), 1)
        run([kn_ref], [vn_ref], (s_i < t_i) & (s_i < n_new))

    run(kc_refs, vc_refs, None)

    @pl.when(j == pl.num_programs(1) - 1)
    def _():
        acc = acc_ref[...]
        for h in range(SB_HEADS):
            o_ref[0, h] = acc[h * SUBLANES:(h + 1) * SUBLANES, h * SB_DIM:(h + 1) * SB_DIM]


def _sb_sample(q_bd, bias_rows, kt_new, vt_new, cache_kt, cache_vt, page_table, n_new):
    bsz = q_bd.shape[0]
    page = cache_kt.shape[3]
    n_pages = page_table.shape[1]
    pps = max(p for p in range(1, SB_SAMPLE_PAGES + 1) if n_pages % p == 0)
    rows = SB_HEADS * SUBLANES
    t_idx = np.arange(page)
    ntri = jnp.asarray(-(t_idx[:, None] > t_idx[None, :]).astype(np.float32), BF16)

    def cache_map(r):
        return lambda b, j, pt: (pt[b * n_pages + (n_pages - 1 - (j * pps + r))], 0, 0, 0)

    per_b3 = lambda b, j, pt: (b, 0, 0)
    per_b4 = lambda b, j, pt: (b, 0, 0, 0)
    page_block = (1, SB_HEADS, SB_DIM, page)
    grid_spec = pltpu.PrefetchScalarGridSpec(
        num_scalar_prefetch=1,
        grid=(bsz, n_pages // pps),
        in_specs=[pl.BlockSpec((1, rows, SB_W), per_b3),
                  pl.BlockSpec((rows, page), lambda b, j, pt: (0, 0)),
                  pl.BlockSpec(page_block, per_b4),
                  pl.BlockSpec(page_block, per_b4)]
                 + [pl.BlockSpec(page_block, cache_map(r)) for r in list(range(pps)) * 2]
                 + [pl.BlockSpec((page, page), lambda b, j, pt: (0, 0))],
        out_specs=pl.BlockSpec((1, SB_HEADS, SUBLANES, SB_DIM), per_b4),
        scratch_shapes=[pltpu.VMEM((rows, SB_W), F32), pltpu.VMEM((rows, page), F32)],
    )
    return pl.pallas_call(
        functools.partial(_sb_sample_kernel, n_new=n_new, page=page, pages_per_step=pps),
        grid_spec=grid_spec,
        out_shape=jax.ShapeDtypeStruct((bsz, SB_HEADS, SUBLANES, SB_DIM), F32),
        compiler_params=_cparams(("parallel", "arbitrary")),
        name="sb_sample",
    )(page_table.reshape(-1), q_bd, bias_rows, kt_new, vt_new, *([cache_kt] * pps), *([cache_vt] * pps), ntri)


def _mix_out_kernel(h_ref, og_ref, gr_ref, os_ref, ggain_ref, sgain_ref, wo_ref, fgain_ref, rw_ref, rb_ref,
                    h1_ref, xn_ref, eidx_ref, gate_ref):
    og = og_ref[...]
    parts = []
    for hh in range(GLA_HEADS):
        x = og[:, hh * GLA_DV:(hh + 1) * GLA_DV]
        parts.append(x * lax.rsqrt(jnp.mean(x * x, axis=-1, keepdims=True) + EPS))
    gr = gr_ref[...]
    og_n = (jnp.concatenate(parts, axis=1) * ggain_ref[...]) * (gr * (1.0 / (1.0 + jnp.exp(-gr))))
    os_n = _half_lane_rms(os_ref[...], sgain_ref[...])
    h1 = h_ref[...] + (_dot(og_n.astype(BF16), wo_ref[:GLA_V_W, :]) + _dot(os_n.astype(BF16), wo_ref[GLA_V_W:, :]))
    h1_ref[...] = h1

    xn = (h1 * lax.rsqrt(jnp.mean(h1 * h1, axis=-1, keepdims=True) + EPS)) * fgain_ref[...]
    for j in range(xn.shape[1] // LANES):
        xn_ref[pl.ds(j, xn.shape[0], stride=SLAB), :] = xn[:, j * LANES:(j + 1) * LANES]

    x_hi, x_lo = _split_bf16(xn)
    rw = rw_ref[...]
    l2 = _dot(x_hi, rw) + _dot(x_lo, rw)
    logits = l2[:, :LANES] + l2[:, LANES:] + rb_ref[...]
    lane = _lane_iota(logits.shape).astype(F32)
    big = jnp.float32(4 * LANES)
    neg = jnp.float32(-jnp.inf)

    is_g = (lane >= N_EXPERTS) & (lane < N_EXPERTS + N_GROUPS)
    lg = jnp.where(is_g, logits, neg)
    mg = jnp.max(lg, axis=-1, keepdims=True)
    g_val = 1.0 / jnp.sum(jnp.exp(lg - mg), axis=-1, keepdims=True)
    g_idx = jnp.min(jnp.where(lg == mg, lane, big), axis=-1, keepdims=True) - N_EXPERTS

    in_g = (lane >= g_idx * EXPERTS_PER_GROUP) & (lane < (g_idx + 1) * EXPERTS_PER_GROUP)
    le = jnp.where(in_g, logits, neg)
    m1 = jnp.max(le, axis=-1, keepdims=True)
    se = jnp.sum(jnp.exp(le - m1), axis=-1, keepdims=True)
    i1 = jnp.min(jnp.where(le == m1, lane, big), axis=-1, keepdims=True)
    le2 = jnp.where(lane == i1, neg, le)
    m2 = jnp.max(le2, axis=-1, keepdims=True)
    i2 = jnp.min(jnp.where(le2 == m2, lane, big), axis=-1, keepdims=True)
    p1 = 1.0 / se
    p2 = jnp.exp(m2 - m1) / se
    tot = p1 + p2
    w1 = g_val * p1 / tot
    w2 = g_val * p2 / tot
    eidx_ref[...] = jnp.where(lane == 0, i1, jnp.where(lane == 1, i2, 0.0))[:, :SUBLANES].astype(jnp.int32)
    gate_ref[...] = jnp.where(lane == 0, w1, jnp.where(lane == 1, w2, 0.0))[:, :SUBLANES]


def _mix_out(h2, og, gr, osb, ggain, sgain, w_out_b, fgain, rw, rb):
    n, d = h2.shape
    tm = min(ROW_TILE, n)
    row = lambda w: pl.BlockSpec((tm, w), lambda i: (i, 0))
    full = lambda a: pl.BlockSpec(a.shape, lambda i: (0,) * a.ndim)
    return pl.pallas_call(
        _mix_out_kernel,
        grid=(n // tm,),
        in_specs=[row(d), row(GLA_V_W), row(GLA_V_W), row(SB_W), full(ggain), full(sgain), full(w_out_b),
                  full(fgain), full(rw), full(rb)],
        out_specs=[row(d), pl.BlockSpec((tm * SLAB, LANES), lambda i: (i, 0)), row(SUBLANES),
                   row(SUBLANES)],
        out_shape=[jax.ShapeDtypeStruct((n, d), F32), jax.ShapeDtypeStruct((n * SLAB, LANES), F32),
                   jax.ShapeDtypeStruct((n, SUBLANES), jnp.int32), jax.ShapeDtypeStruct((n, SUBLANES), F32)],
        compiler_params=_cparams(("parallel",)),
        name="mix_out",
    )(h2, og, gr, osb, ggain, sgain, w_out_b, fgain, rw, rb)


def _moe_slots_kernel(eidx_ref, ltri_ref, dest_ref, cnt_ref, counts_ref, run_ref, pstart_ref):
    ph = pl.program_id(0)
    i = pl.program_id(1)
    e = eidx_ref[...]
    tm = e.shape[0]
    lane = _lane_iota((tm, LANES))
    oh0 = (lane == e[:, 0:1]).astype(F32)
    oh1 = (lane == e[:, 1:2]).astype(F32)
    tot0 = jnp.sum(oh0, axis=0, keepdims=True)
    tot1 = jnp.sum(oh1, axis=0, keepdims=True)

    @pl.when((ph == 0) & (i == 0))
    def _():
        counts_ref[...] = jnp.zeros_like(counts_ref)

    @pl.when(ph == 0)
    def _():
        counts_ref[...] += tot0 + tot1

    @pl.when((ph == 1) & (i == 0))
    def _():
        cnt = counts_ref[...]
        padded = jnp.floor((cnt + (MOE_BLOCK - 1)) * (1.0 / MOE_BLOCK)) * MOE_BLOCK
        x = jnp.broadcast_to(padded, (SUBLANES, LANES))
        l8 = _lane_iota((SUBLANES, LANES))
        s = 1
        while s < LANES:
            x = x + jnp.where(l8 >= s, pltpu.roll(x, s, axis=1), 0.0)
            s *= 2
        pstart_ref[...] = x[0:1] - padded
        run_ref[...] = jnp.zeros_like(run_ref)
        cnt_ref[...] = jnp.broadcast_to(cnt, (SUBLANES, LANES)).astype(jnp.int32)

    @pl.when(ph == 1)
    def _():
        ltri = ltri_ref[...]
        base0 = run_ref[...] + pstart_ref[...]
        c0 = _dot(ltri, oh0.astype(BF16))
        c1 = _dot(ltri, oh1.astype(BF16))
        d0 = jnp.sum(oh0 * (base0 + c0), axis=-1, keepdims=True)
        d1 = jnp.sum(oh1 * (base0 + tot0 + c1), axis=-1, keepdims=True)
        l8 = _lane_iota((tm, LANES))
        dest_ref[...] = jnp.where(l8 == 0, d0, jnp.where(l8 == 1, d1, 0.0))[:, :SUBLANES].astype(jnp.int32)
        run_ref[...] += tot0 + tot1


def _moe_slots(eidx):
    n = eidx.shape[0]
    tm = MOE_TILE
    t_idx = np.arange(tm)
    ltri = jnp.asarray((t_idx[None, :] < t_idx[:, None]).astype(np.float32), BF16)
    return pl.pallas_call(
        _moe_slots_kernel,
        grid=(2, n // tm),
        in_specs=[pl.BlockSpec((tm, SUBLANES), lambda ph, i: (i, 0)),
                  pl.BlockSpec((tm, tm), lambda ph, i: (0, 0))],
        out_specs=[pl.BlockSpec((tm, SUBLANES), lambda ph, i: (i * ph, 0)),
                   pl.BlockSpec((SUBLANES, LANES), lambda ph, i: (0, 0))],
        out_shape=[jax.ShapeDtypeStruct((n, SUBLANES), jnp.int32),
                   jax.ShapeDtypeStruct((SUBLANES, LANES), jnp.int32)],
        scratch_shapes=[pltpu.VMEM((1, LANES), F32)] * 3,
        compiler_params=_cparams(("arbitrary", "arbitrary")),
        name="moe_slots",
    )(eidx, ltri)


def _load_slots(d0_hbm, d1_hbm, d0_s, d1_s, isem, tile, tm):
    base = pl.multiple_of(tile * tm, tm)
    c0 = pltpu.make_async_copy(d0_hbm.at[pl.ds(base, tm)], d0_s, isem.at[0])
    c1 = pltpu.make_async_copy(d1_hbm.at[pl.ds(base, tm)], d1_s, isem.at[1])
    c0.start()
    c1.start()
    c0.wait()
    c1.wait()


def _on_parity(i, fn):
    for s in (0, 1):
        pl.when(i % 2 == s)(functools.partial(fn, s))


def _moe_dispatch_kernel(cnt_ref, end_ref, d0_hbm, d1_hbm, x_ref, xs_hbm, d0_s, d1_s, zero_ref, isem, zsem, sem,
                         *, tm):
    i = pl.program_id(0)

    @pl.when(i == 0)
    def _():
        zero_ref[...] = jnp.zeros_like(zero_ref)

        def clear(e):
            first = pl.multiple_of((end_ref[e] - MOE_BLOCK) * SLAB, SLAB)
            return pltpu.make_async_copy(zero_ref, xs_hbm.at[pl.ds(first, MOE_BLOCK * SLAB)], zsem)

        for e in range(N_EXPERTS):
            pl.when(cnt_ref[e] > 0)(lambda e=e: clear(e).start())
        for e in range(N_EXPERTS):
            pl.when(cnt_ref[e] > 0)(lambda e=e: clear(e).wait())

        used = end_ref[N_EXPERTS - 1] // MOE_BLOCK
        total = xs_hbm.shape[0] // (MOE_BLOCK * SLAB)

        def spare(b):
            first = pl.multiple_of(b * (MOE_BLOCK * SLAB), SLAB)
            return pltpu.make_async_copy(zero_ref, xs_hbm.at[pl.ds(first, MOE_BLOCK * SLAB)], zsem)

        lax.fori_loop(used, total, lambda b, c: (spare(b).start(), c)[1], 0)
        lax.fori_loop(used, total, lambda b, c: (spare(b).wait(), c)[1], 0)

    _load_slots(d0_hbm, d1_hbm, d0_s, d1_s, isem, i, tm)

    def copies(r):
        src = x_ref.at[_slab(r)]
        return (pltpu.make_async_copy(src, xs_hbm.at[_slab(d0_s[r])], sem.at[0]),
                pltpu.make_async_copy(src, xs_hbm.at[_slab(d1_s[r])], sem.at[1]))

    def issue(r, carry):
        a, b = copies(r)
        a.start()
        b.start()
        return carry

    def drain(r, carry):
        a, b = copies(r)
        a.wait()
        b.wait()
        return carry

    lax.fori_loop(0, tm, issue, 0, unroll=DMA_LOOP_UNROLL)
    lax.fori_loop(0, tm, drain, 0, unroll=DMA_LOOP_UNROLL)


def _moe_dispatch(counts, end_rows, d0, d1, xn_slab, m_pad):
    n = xn_slab.shape[0] // SLAB
    tm = MOE_TILE
    grid_spec = pltpu.PrefetchScalarGridSpec(
        num_scalar_prefetch=2,
        grid=(n // tm,),
        in_specs=[pl.BlockSpec(memory_space=pl.ANY), pl.BlockSpec(memory_space=pl.ANY),
                  pl.BlockSpec((tm * SLAB, LANES), lambda i, cnt, end: (i, 0))],
        out_specs=pl.BlockSpec(memory_space=pl.ANY),
        scratch_shapes=[pltpu.SMEM((tm,), jnp.int32), pltpu.SMEM((tm,), jnp.int32),
                        pltpu.VMEM((MOE_BLOCK * SLAB, LANES), F32),
                        pltpu.SemaphoreType.DMA((2,)), pltpu.SemaphoreType.DMA(()),
                        pltpu.SemaphoreType.DMA((2,))],
    )
    return pl.pallas_call(
        functools.partial(_moe_dispatch_kernel, tm=tm),
        grid_spec=grid_spec,
        out_shape=jax.ShapeDtypeStruct((m_pad * SLAB, LANES), F32),
        compiler_params=_cparams(("arbitrary",)),
        name="moe_dispatch",
    )(counts, end_rows, d0, d1, xn_slab)


def _moe_experts_kernel(be_ref, nu_ref, xs_ref, wg_ref, wu_ref, wd_ref, y_ref, wgb_ref, wub_ref, wdb_ref):
    i = pl.program_id(0)
    rows = xs_ref.shape[0] // SLAB

    @pl.when((i == 0) | (be_ref[i] != be_ref[jnp.maximum(i - 1, 0)]))
    def _():
        wgb_ref[...] = wg_ref[0].astype(BF16)
        wub_ref[...] = wu_ref[0].astype(BF16)
        wdb_ref[...] = wd_ref[0].astype(BF16)

    @pl.when(i < nu_ref[0])
    def _():
        x = jnp.concatenate([xs_ref[pl.ds(j, rows, stride=SLAB), :] for j in range(SLAB)], axis=1).astype(BF16)
        g = _dot(x, wgb_ref[...])
        u = _dot(x, wub_ref[...])
        hdn = (g * (1.0 / (1.0 + jnp.exp(-g)))) * u
        y = _dot(hdn.astype(BF16), wdb_ref[...])
        for j in range(SLAB):
            y_ref[pl.ds(j, rows, stride=SLAB), :] = y[:, j * LANES:(j + 1) * LANES]

    @pl.when(i >= nu_ref[0])
    def _():
        y_ref[...] = jnp.zeros_like(y_ref)


def _moe_experts(block_e, n_used, xs, wg, wu, wd):
    m_pad = xs.shape[0] // SLAB
    nb = m_pad // MOE_BLOCK
    d, de = wg.shape[1], wg.shape[2]
    rows = pl.BlockSpec((MOE_BLOCK * SLAB, LANES), lambda i, be, nu: (i, 0))
    grid_spec = pltpu.PrefetchScalarGridSpec(
        num_scalar_prefetch=2,
        grid=(nb,),
        in_specs=[pl.BlockSpec((MOE_BLOCK * SLAB, LANES), lambda i, be, nu: (jnp.minimum(i, nu[0] - 1), 0)),
                  pl.BlockSpec((1, d, de), lambda i, be, nu: (be[i], 0, 0)),
                  pl.BlockSpec((1, d, de), lambda i, be, nu: (be[i], 0, 0)),
                  pl.BlockSpec((1, de, d), lambda i, be, nu: (be[i], 0, 0))],
        out_specs=rows,
        scratch_shapes=[pltpu.VMEM((d, de), BF16), pltpu.VMEM((d, de), BF16), pltpu.VMEM((de, d), BF16)],
    )
    return pl.pallas_call(
        _moe_experts_kernel,
        grid_spec=grid_spec,
        out_shape=jax.ShapeDtypeStruct((m_pad * SLAB, LANES), F32),
        compiler_params=_cparams(("arbitrary",)),
        name="moe_experts",
    )(block_e, n_used, xs, wg, wu, wd)


def _moe_combine_kernel(d0_hbm, d1_hbm, gate_ref, h1_ref, y_hbm, out_ref, d0_a, d1_a, d0_b, d1_b,
                        buf0_a, buf1_a, buf0_b, buf1_b, isem, sem, *, tm):
    i = pl.program_id(0)
    last = pl.num_programs(0) - 1
    idx = ((d0_a, d1_a), (d0_b, d1_b))
    bufs = ((buf0_a, buf1_a), (buf0_b, buf1_b))
    cr = MOE_COMBINE_ROWS

    def copies(s, r):
        return (pltpu.make_async_copy(y_hbm.at[_slab(idx[s][0][r])], bufs[s][0].at[_slab(r)], sem.at[s, 0]),
                pltpu.make_async_copy(y_hbm.at[_slab(idx[s][1][r])], bufs[s][1].at[_slab(r)], sem.at[s, 1]))

    def gather(tile, s):
        _load_slots(d0_hbm, d1_hbm, idx[s][0], idx[s][1], isem, tile, tm)

        def body(r, carry):
            a, b = copies(s, r)
            a.start()
            b.start()
            return carry
        lax.fori_loop(0, tm, body, 0, unroll=DMA_LOOP_UNROLL)

    def run(s):
        @pl.when(i == 0)
        def _():
            gather(i, s)

        @pl.when(i < last)
        def _():
            gather(i + 1, 1 - s)

        def drain(r, carry):
            a, b = copies(s, r)
            a.wait()
            b.wait()
            return carry
        lax.fori_loop(0, tm, drain, 0, unroll=DMA_LOOP_UNROLL)

        def chunk(c, carry):
            r0 = pl.multiple_of(c * cr, cr)
            rows = pl.ds(r0, cr)
            w0 = jnp.broadcast_to(gate_ref[rows, 0:1], (cr, LANES))
            w1 = jnp.broadcast_to(gate_ref[rows, 1:2], (cr, LANES))
            for j in range(SLAB):
                srows = pl.ds(r0 * SLAB + j, cr, stride=SLAB)
                y = bufs[s][0][srows, :] * w0 + bufs[s][1][srows, :] * w1
                out_ref[rows, j * LANES:(j + 1) * LANES] = h1_ref[rows, j * LANES:(j + 1) * LANES] + y
            return carry
        lax.fori_loop(0, tm // cr, chunk, 0)

    _on_parity(i, run)


def _moe_combine(d0, d1, gate, h1, y_rows):
    n, d = h1.shape
    tm = MOE_TILE
    return pl.pallas_call(
        functools.partial(_moe_combine_kernel, tm=tm),
        grid=(n // tm,),
        in_specs=[pl.BlockSpec(memory_space=pl.ANY), pl.BlockSpec(memory_space=pl.ANY),
                  pl.BlockSpec((tm, SUBLANES), lambda i: (i, 0)),
                  pl.BlockSpec((tm, d), lambda i: (i, 0)),
                  pl.BlockSpec(memory_space=pl.ANY)],
        out_specs=pl.BlockSpec((tm, d), lambda i: (i, 0)),
        out_shape=jax.ShapeDtypeStruct((n, d), F32),
        scratch_shapes=[pltpu.SMEM((tm,), jnp.int32)] * 4 + [pltpu.VMEM((tm * SLAB, LANES), F32)] * 4
                       + [pltpu.SemaphoreType.DMA((2,)), pltpu.SemaphoreType.DMA((2, 2))],
        compiler_params=_cparams(("arbitrary",)),
        name="moe_combine",
    )(d0, d1, gate, h1, y_rows)


def _moe(h1, xn_slab, eidx, gate, wg, wu, wd):
    n = h1.shape[0]
    assert h1.shape[1] == SLAB * LANES
    n_pad = -(-n // MOE_TILE) * MOE_TILE
    if n_pad != n:
        padr = lambda a, k: jnp.pad(a, ((0, k * (n_pad - n)),) + ((0, 0),) * (a.ndim - 1))
        h1, xn_slab, eidx, gate = padr(h1, 1), padr(xn_slab, SLAB), padr(eidx, 1), padr(gate, 1)
    dest, cnt = _moe_slots(eidx)
    d0, d1 = dest[:, 0], dest[:, 1]
    nb = (2 * n_pad) // MOE_BLOCK + N_EXPERTS
    counts = cnt[0, :N_EXPERTS]
    pend = jnp.cumsum((counts + MOE_BLOCK - 1) // MOE_BLOCK)
    block_e = jnp.minimum(jnp.sum(pend[None, :] <= jnp.arange(nb, dtype=jnp.int32)[:, None], axis=1),
                          N_EXPERTS - 1).astype(jnp.int32)
    n_used = pend[-1:].astype(jnp.int32)
    xs = _moe_dispatch(counts, (pend * MOE_BLOCK).astype(jnp.int32), d0, d1, xn_slab, nb * MOE_BLOCK)
    y_rows = _moe_experts(block_e, n_used, xs, wg, wu, wd)
    return _moe_combine(d0, d1, gate, h1, y_rows)[:n]


def kernel(x_prompt, x_sample, cache_sb_k, cache_sb_v, state_gla, page_table, meta_tokens, norm_mix_gain, w_in, gla_w_alpha, gla_b_alpha, gla_out_gain, sb_q_gain, sb_k_gain, sb_logit_bias, sb_out_gain, w_out, norm_ffn_gain, router_group, router_group_b, router_expert, router_expert_b, w_gate, w_up, w_down):
    bsz, seq, d = x_prompt.shape
    dbs, dseq, _ = x_sample.shape
    depth = w_in.shape[0]
    page = cache_sb_k.shape[2]
    t_real = seq + N_META
    t_pad = -(-t_real // SB_TQ) * SB_TQ
    fpad = t_pad - t_real

    hp = jnp.concatenate([jnp.zeros((bsz, fpad, d), x_prompt.dtype),
                          jnp.broadcast_to(meta_tokens[None].astype(x_prompt.dtype), (bsz, N_META, d)),
                          x_prompt], axis=1).reshape(bsz * t_pad, d)
    hs = x_sample.reshape(dbs * dseq, d)

    outs = {k: [] for k in ("kp", "vp", "sp", "ks", "vs", "ss")}
    for l in range(depth):
        w = w_in[l]
        w_in_r = jnp.concatenate([w[:, :1536], w[:, 1552:3088], w[:, 1536:1552],
                                  jnp.zeros((d, _W_IN_COLS - 3088), w.dtype)], axis=1).astype(BF16)
        wa_pad = jnp.pad(gla_w_alpha[l], ((0, LANES - GLA_RANK), (0, 0)))
        ba = gla_b_alpha[l][None]
        norm_g = norm_mix_gain[l][None]
        qgain = jnp.tile(sb_q_gain[l], SB_HEADS)[None]
        kgain = jnp.tile(sb_k_gain[l], SB_HEADS)[None]
        ggain = jnp.tile(gla_out_gain[l], GLA_HEADS)[None]
        sgain = jnp.tile(sb_out_gain[l], SB_HEADS)[None]
        fgain = norm_ffn_gain[l][None]
        w_out_b = w_out[l].astype(BF16)
        r_all = jnp.concatenate([router_expert[l].transpose(1, 0, 2).reshape(d, N_EXPERTS), router_group[l],
                                 jnp.zeros((d, LANES - N_EXPERTS - N_GROUPS), F32)], axis=1)
        r_hi = r_all.astype(BF16)
        r_lo = (r_all - r_hi.astype(F32)).astype(BF16)
        rw = jnp.concatenate([r_hi, r_lo], axis=1)
        rb = jnp.concatenate([router_expert_b[l].reshape(N_EXPERTS), router_group_b[l],
                              jnp.zeros((LANES - N_EXPERTS - N_GROUPS,), F32)])[None]
        wg, wu, wd = w_gate[l], w_up[l], w_down[l]
        bias2 = sb_logit_bias[l].astype(F32) * LOG2E
        b_hi = bias2.astype(BF16).astype(F32)
        b_lo = (bias2 - b_hi).astype(BF16).astype(F32)
        bias_hl = jnp.stack([b_hi, b_lo], axis=1).reshape(-1)

        gq, gk, gv, gr, la, qs, ks, vs, ksb, vsb = _in_proj(hp, norm_g, w_in_r, wa_pad, ba, qgain, kgain)
        b3 = lambda a: a.reshape(bsz, t_pad, a.shape[-1])
        o_g, s_p = _gla(b3(gq), b3(gk), b3(gv), b3(la), jnp.zeros((bsz, GLA_QK_W, GLA_DV), F32), fpad)
        o_s = _sb_prompt(b3(qs), b3(ksb), b3(vsb), bias_hl)
        h1, xn3, eidx, gate = _mix_out(hp, o_g.reshape(-1, GLA_V_W), gr, o_s.reshape(-1, SB_W), ggain, sgain,
                                       w_out_b, fgain, rw, rb)
        hp = _moe(h1, xn3, eidx, gate, wg, wu, wd)
        outs["kp"].append(b3(ks)[:, fpad:].reshape(bsz, t_real, SB_HEADS, SB_DIM))
        outs["vp"].append(b3(vs)[:, fpad:].reshape(bsz, t_real, SB_HEADS, SB_DIM))
        outs["sp"].append(s_p.reshape(bsz, GLA_HEADS, GLA_DK, GLA_DV))

        gq, gk, gv, gr, la, qs, ks, vs, ksb, vsb = _in_proj(hs, norm_g, w_in_r, wa_pad, ba, qgain, kgain)
        cpad = GLA_CHUNK - dseq
        c3 = lambda a: jnp.pad(a.reshape(dbs, dseq, a.shape[-1]), ((0, 0), (cpad, 0), (0, 0)))
        o_g, s_s = _gla(c3(gq), c3(gk), c3(gv), c3(la), state_gla[l].reshape(dbs, GLA_QK_W, GLA_DV), cpad)
        o_g = o_g[:, cpad:].reshape(dbs * dseq, GLA_V_W)
        q4 = qs.reshape(dbs, dseq, SB_HEADS, SB_DIM).transpose(0, 2, 1, 3)
        q4 = jnp.pad(q4, ((0, 0), (0, 0), (0, SUBLANES - dseq), (0, 0)))
        eye = jnp.eye(SB_HEADS, dtype=q4.dtype)
        q_bd = (q4[:, :, :, None, :] * eye[None, :, None, :, None]).reshape(dbs, SB_HEADS * SUBLANES, SB_W)
        bias_rows = jnp.broadcast_to(jnp.repeat(bias2, SUBLANES)[:, None], (SB_HEADS * SUBLANES, page))
        to_t = lambda a: jnp.pad(a.reshape(dbs, dseq, SB_HEADS, SB_DIM).transpose(0, 2, 3, 1),
                                 ((0, 0), (0, 0), (0, 0), (0, page - dseq)))
        cache_kt = cache_sb_k[l].transpose(0, 2, 3, 1)
        cache_vt = cache_sb_v[l].transpose(0, 2, 3, 1)
        o4 = _sb_sample(q_bd, bias_rows, to_t(ks), to_t(vs), cache_kt, cache_vt, page_table, dseq)
        o_s = o4[:, :, :dseq].transpose(0, 2, 1, 3).reshape(dbs * dseq, SB_W)
        h1, xn3, eidx, gate = _mix_out(hs, o_g, gr, o_s, ggain, sgain, w_out_b, fgain, rw, rb)
        hs = _moe(h1, xn3, eidx, gate, wg, wu, wd)
        outs["ks"].append(ks.reshape(dbs, dseq, SB_HEADS, SB_DIM))
        outs["vs"].append(vs.reshape(dbs, dseq, SB_HEADS, SB_DIM))
        outs["ss"].append(s_s.reshape(dbs, GLA_HEADS, GLA_DK, GLA_DV))

    y_prompt = hp.reshape(bsz, t_pad, d)[:, fpad + N_META:]
    y_sample = hs.reshape(dbs, dseq, d)
    return (y_prompt, y_sample, jnp.stack(outs["kp"]), jnp.stack(outs["vp"]), jnp.stack(outs["sp"]),
            jnp.stack(outs["ks"]), jnp.stack(outs["vs"]), jnp.stack(outs["ss"]))
```

```python
import functools

import jax
import jax.numpy as jnp
import numpy as np
from jax import lax
from jax.experimental import pallas as pl
from jax.experimental.pallas import tpu as pltpu

F32 = jnp.float32
BF16 = jnp.bfloat16

N_META = 16
GLA_HEADS = 4
GLA_DK = 64
GLA_DV = 128
GLA_RANK = 16
GLA_TAU = 16.0
GLA_QK_W = GLA_HEADS * GLA_DK
GLA_V_W = GLA_HEADS * GLA_DV
SB_HEADS = 8
SB_DIM = 64
SB_W = SB_HEADS * SB_DIM
N_GROUPS = 4
EXPERTS_PER_GROUP = 8
N_EXPERTS = N_GROUPS * EXPERTS_PER_GROUP
EPS = 1e-6
LOG2E = 1.4426950408889634

LANES = 128
SUBLANES = 8
VMEM_LIMIT_BYTES = 56 * 1024 * 1024

ROW_TILE = 512
GLA_CHUNK = 128
GLA_BATCH = 2
SB_TQ = 768
SB_TK = 256
SB_TRIP_BLOCKS = 2
SB_SAMPLE_PAGES = 16
MOE_TILE = 1024
MOE_BLOCK = 512
MOE_COMBINE_ROWS = 128
DMA_LOOP_UNROLL = 8
SLAB = 8

_C_GQ, _C_GK, _C_GV, _C_GR, _C_SQ, _C_SK, _C_SV, _C_GA = 0, 256, 512, 1024, 1536, 2048, 2560, 3072
_W_IN_COLS = 3200


def _cparams(sem):
    return pltpu.CompilerParams(dimension_semantics=sem, vmem_limit_bytes=VMEM_LIMIT_BYTES)


def _dot(a, b):
    return jnp.dot(a, b, preferred_element_type=F32)


def _dot_nt(a, b):
    return lax.dot_general(a, b, (((1,), (1,)), ((), ())), preferred_element_type=F32)


def _split_bf16(x):
    hi = x.astype(BF16)
    lo = (x - hi.astype(F32)).astype(BF16)
    return hi, lo


def _slab(r):
    return pl.ds(pl.multiple_of(r * SLAB, SLAB), SLAB)


def _lane_iota(shape):
    return lax.broadcasted_iota(jnp.int32, shape, len(shape) - 1)


def _half_lane_rms(x, gain):
    outs = []
    for g in range(x.shape[1] // LANES):
        xg = x[:, g * LANES:(g + 1) * LANES]
        x2 = xg * xg
        low = _lane_iota(xg.shape) < SB_DIM
        s_all = jnp.sum(x2, axis=-1, keepdims=True)
        s_lo = jnp.sum(jnp.where(low, x2, 0.0), axis=-1, keepdims=True)
        ms = jnp.where(low, s_lo, s_all - s_lo) * (1.0 / SB_DIM)
        outs.append(xg * lax.rsqrt(ms + EPS))
    return jnp.concatenate(outs, axis=1) * gain


def _in_proj_kernel(x_ref, g_ref, w_ref, wa_ref, ba_ref, qgain_ref, kgain_ref,
                    gq_ref, gk_ref, gv_ref, gr_ref, la_ref, qs_ref, ks_ref, vs_ref, ksb_ref, vsb_ref):
    x = x_ref[...]
    ms = jnp.mean(x * x, axis=-1, keepdims=True)
    xn = ((x * lax.rsqrt(ms + EPS)) * g_ref[...]).astype(BF16)

    def proj(c0, width):
        return _dot(xn, w_ref[:, c0:c0 + width])

    gq_ref[...] = proj(_C_GQ, GLA_QK_W) * (GLA_DK ** -0.5)
    gk_ref[...] = proj(_C_GK, GLA_QK_W)
    gv_ref[...] = proj(_C_GV, GLA_V_W)
    gr_ref[...] = proj(_C_GR, GLA_V_W)

    ga_hi, ga_lo = _split_bf16(proj(_C_GA, LANES))
    wa_hi, wa_lo = _split_bf16(wa_ref[...])
    u = _dot(ga_hi, wa_hi) + _dot(ga_lo, wa_hi) + _dot(ga_hi, wa_lo) + ba_ref[...]
    la_ref[...] = (jnp.minimum(u, 0.0) - jnp.log(1.0 + jnp.exp(-jnp.abs(u)))) * (1.0 / GLA_TAU)

    q_s = _half_lane_rms(proj(_C_SQ, SB_W), qgain_ref[...])
    qs_ref[...] = (q_s * (SB_DIM ** -0.5 * LOG2E)).astype(BF16)
    k_s = _half_lane_rms(proj(_C_SK, SB_W), kgain_ref[...])
    ks_ref[...] = k_s
    ksb_ref[...] = k_s.astype(BF16)
    v_s = proj(_C_SV, SB_W)
    vs_ref[...] = v_s
    vsb_ref[...] = v_s.astype(BF16)


def _in_proj(h2, norm_g, w_in_r, wa_pad, ba, qgain, kgain):
    n, d = h2.shape
    tm = min(ROW_TILE, n)
    row = lambda w: pl.BlockSpec((tm, w), lambda i: (i, 0))
    full = lambda a: pl.BlockSpec(a.shape, lambda i: (0,) * a.ndim)
    outs = [(GLA_QK_W, F32), (GLA_QK_W, F32), (GLA_V_W, F32), (GLA_V_W, F32), (GLA_QK_W, F32),
            (SB_W, BF16), (SB_W, F32), (SB_W, F32), (SB_W, BF16), (SB_W, BF16)]
    return pl.pallas_call(
        _in_proj_kernel,
        grid=(n // tm,),
        in_specs=[row(d), full(norm_g), full(w_in_r), full(wa_pad), full(ba), full(qgain), full(kgain)],
        out_specs=[row(w) for w, _ in outs],
        out_shape=[jax.ShapeDtypeStruct((n, w), dt) for w, dt in outs],
        compiler_params=_cparams(("parallel",)),
        name="in_proj",
    )(h2, norm_g, w_in_r, wa_pad, ba, qgain, kgain)


def _gla_levels(c):
    levels = []
    l = c // 2
    while l >= 1:
        levels.append(l)
        l //= 2
    return levels


def _gla_constants(c):
    t = np.arange(c)
    tri = (t[None, :] <= t[:, None]).astype(np.float32)
    mats, masks = [tri], []
    for l in _gla_levels(c):
        mid = (t // (2 * l)) * (2 * l) + l
        if l < SUBLANES:
            mats.append((t[None, :] <= (mid[:, None] - 1)).astype(np.float32))
        same = (t[:, None] // (2 * l)) == (t[None, :] // (2 * l))
        masks.append((same & ((t[:, None] % (2 * l)) >= l) & ((t[None, :] % (2 * l)) < l)).astype(np.float32))
    masks.append(np.eye(c, dtype=np.float32))
    return np.concatenate(mats, axis=0), np.stack(masks)


def _gla_kernel(q_ref, k_ref, v_ref, la_ref, s0_ref, gmat_ref, mask_ref, o_ref, s_out_ref, st_ref,
                *, chunk, front_pad):
    for bb in range(q_ref.shape[0]):
        _gla_chunk(bb, q_ref, k_ref, v_ref, la_ref, s0_ref, gmat_ref, mask_ref, o_ref, s_out_ref, st_ref,
                   chunk, front_pad)


def _gla_chunk(bb, q_ref, k_ref, v_ref, la_ref, s0_ref, gmat_ref, mask_ref, o_ref, s_out_ref, st_ref,
               chunk, front_pad):
    c = pl.program_id(1)
    levels = _gla_levels(chunk)
    w = GLA_QK_W

    @pl.when(c == 0)
    def _():
        st_ref[bb] = s0_ref[bb].T

    q = q_ref[bb]
    k = k_ref[bb]
    la = la_ref[bb]
    if front_pad:
        row = lax.broadcasted_iota(jnp.int32, la.shape, 0) + c * chunk
        la = jnp.where(row < front_pad, 0.0, la)
    la_hi, la_lo = _split_bf16(la)
    p = _dot(gmat_ref[...], jnp.concatenate([la_hi, la_lo], axis=1))
    p = p[:, :w] + p[:, w:]
    b = p[:chunk]
    lane = _lane_iota((1, w))
    head_masks = [(lane >= h * GLA_DK) & (lane < (h + 1) * GLA_DK) for h in range(GLA_HEADS)]

    scores = [jnp.zeros((chunk, chunk), F32) for _ in range(GLA_HEADS)]
    n_mat = 0
    for i, l in enumerate(levels):
        if l < SUBLANES:
            n_mat += 1
            r = p[n_mat * chunk:(n_mat + 1) * chunk]
        else:
            r = jnp.concatenate([jnp.broadcast_to(b[g + l - 1:g + l], (2 * l, w))
                                 for g in range(0, chunk, 2 * l)], axis=0)
        qt = q * jnp.exp(jnp.minimum(b - r, 0.0))
        kt = (k * jnp.exp(jnp.minimum(r - b, 0.0))).astype(BF16)
        for h in range(GLA_HEADS):
            qh = jnp.where(head_masks[h], qt, 0.0).astype(BF16)
            scores[h] = scores[h] + mask_ref[i] * _dot_nt(qh, kt)
    kb = k.astype(BF16)
    for h in range(GLA_HEADS):
        qh = jnp.where(head_masks[h], q, 0.0).astype(BF16)
        scores[h] = scores[h] + mask_ref[len(levels)] * _dot_nt(qh, kb)

    st = st_ref[bb]
    st_b = st.astype(BF16)
    b_last = b[chunk - 1:chunk]
    q_in = q * jnp.exp(b)
    k_out = k * jnp.exp(b_last - b)
    upd = jnp.zeros_like(st)
    for h in range(GLA_HEADS):
        vh = v_ref[bb, :, h * GLA_DV:(h + 1) * GLA_DV]
        vhb = vh.astype(BF16)
        qh = jnp.where(head_masks[h], q_in, 0.0).astype(BF16)
        o_ref[bb, :, h * GLA_DV:(h + 1) * GLA_DV] = _dot(scores[h].astype(BF16), vhb) + _dot_nt(qh, st_b)
        kh = jnp.where(head_masks[h], k_out, 0.0).astype(BF16)
        upd = upd + _dot(vh.T.astype(BF16), kh)
    st_new = st * jnp.exp(b_last) + upd
    st_ref[bb] = st_new

    @pl.when(c == pl.num_programs(1) - 1)
    def _():
        s_out_ref[bb] = st_new.T


def _gla(q, k, v, la, s0, front_pad):
    bsz, t, _ = q.shape
    chunk = GLA_CHUNK
    gmat, masks = _gla_constants(chunk)
    gmat = jnp.asarray(gmat, BF16)
    masks = jnp.asarray(masks, F32)
    nb = GLA_BATCH
    assert bsz % nb == 0
    tok = lambda w: pl.BlockSpec((nb, chunk, w), lambda b, c: (b, c, 0))
    per_b = pl.BlockSpec((nb, GLA_QK_W, GLA_DV), lambda b, c: (b, 0, 0))
    return pl.pallas_call(
        functools.partial(_gla_kernel, chunk=chunk, front_pad=front_pad),
        grid=(bsz // nb, t // chunk),
        in_specs=[tok(GLA_QK_W), tok(GLA_QK_W), tok(GLA_V_W), tok(GLA_QK_W), per_b,
                  pl.BlockSpec(gmat.shape, lambda b, c: (0, 0)),
                  pl.BlockSpec(masks.shape, lambda b, c: (0, 0, 0))],
        out_specs=[tok(GLA_V_W), per_b],
        out_shape=[jax.ShapeDtypeStruct((bsz, t, GLA_V_W), F32),
                   jax.ShapeDtypeStruct((bsz, GLA_QK_W, GLA_DV), F32)],
        scratch_shapes=[pltpu.VMEM((nb, GLA_DV, GLA_QK_W), F32)],
        compiler_params=_cparams(("parallel", "arbitrary")),
        name="gla",
    )(q, k, v, la, s0, gmat, masks)


def _softplus2(z):
    neg_abs = lax.bitcast_convert_type(lax.bitcast_convert_type(z, jnp.uint32) | jnp.uint32(0x80000000), F32)
    return jnp.maximum(z, 0.0) + jnp.log2(1.0 + jnp.exp2(neg_abs))


def _sb_prompt_kernel(bias_ref, q_ref, k_ref, v_ref, ntri_ref, o_ref, acc_ref, c0_ref, c1_ref, *, tq, tk):
    hp = pl.program_id(1)
    i = pl.program_id(2)
    ndiag = tq // tk
    q = q_ref[0].astype(F32)
    lane_q = _lane_iota(q.shape)
    q0 = jnp.where(lane_q < SB_DIM, q, jnp.where(lane_q < SB_DIM + 2, 1.0, 0.0)).astype(BF16)
    q1 = jnp.where(lane_q >= SB_DIM, q, jnp.where(lane_q < 2, 1.0, 0.0)).astype(BF16)
    ntri = ntri_ref[...]
    lane_k = _lane_iota((tk, LANES))
    low_k = lane_k < SB_DIM

    def bias_lanes(h, first_lane):
        hi = jnp.full((tk, LANES), bias_ref[4 * hp + 2 * h], F32)
        lo = jnp.full((tk, LANES), bias_ref[4 * hp + 2 * h + 1], F32)
        return jnp.where(lane_k == first_lane, hi, jnp.where(lane_k == first_lane + 1, lo, 0.0)).astype(BF16)

    kbias0 = bias_lanes(0, SB_DIM)
    kbias1 = bias_lanes(1, 0)

    def head(qh, kh, vh, carry, mask):
        z = _dot_nt(qh, kh)
        sp = _softplus2(z)
        if mask is not None:
            sp = jnp.where(mask, sp, 0.0)
        between = _dot(sp.astype(BF16), ntri)
        a = jnp.exp2((z - sp) + between + jnp.concatenate([carry] * (tk // LANES), axis=1))
        if mask is not None:
            a = jnp.where(mask, a, 0.0)
        rowsum = jnp.broadcast_to(jnp.sum(sp, axis=-1, keepdims=True), carry.shape)
        return _dot(a.astype(BF16), vh), carry - rowsum

    def tile(kt, r0, mask, c0, c1):
        rows = pl.ds(pl.multiple_of(kt * tk, tk), tk)
        kb = k_ref[0, rows, :]
        vb = v_ref[0, rows, :]
        zk = jnp.zeros_like(vb)
        o0, c0 = head(q0[r0:], jnp.where(low_k, kb, kbias0), jnp.where(low_k, vb, zk), c0, mask)
        o1, c1 = head(q1[r0:], jnp.where(low_k, kbias1, kb), jnp.where(low_k, zk, vb), c1, mask)
        return o0 + o1, c0, c1

    nfull = i * ndiag

    acc = jnp.zeros((tq, LANES), F32)
    c0 = jnp.zeros((tq, LANES), F32)
    c1 = jnp.zeros((tq, LANES), F32)
    for d in reversed(range(ndiag)):
        r0 = d * tk
        nr = tq - r0
        mask = lax.broadcasted_iota(jnp.int32, (nr, tk), 0) > lax.broadcasted_iota(jnp.int32, (nr, tk), 1)
        o, c0n, c1n = tile(nfull + d, r0, mask, c0[r0:], c1[r0:])
        tail_rows = lambda full, new: new if r0 == 0 else jnp.concatenate([full[:r0], new], axis=0)
        acc = tail_rows(acc, acc[r0:] + o)
        c0 = tail_rows(c0, c0n)
        c1 = tail_rows(c1, c1n)
    acc_ref[...] = acc
    c0_ref[...] = c0
    c1_ref[...] = c1

    def trip(first_kt, ntiles):
        c0 = c0_ref[...]
        c1 = c1_ref[...]
        total = None
        for u in range(ntiles):
            o, c0, c1 = tile(first_kt - u, 0, None, c0, c1)
            total = o if total is None else total + o
        acc_ref[...] += total
        c0_ref[...] = c0
        c1_ref[...] = c1

    per_trip = SB_TRIP_BLOCKS * ndiag

    def body(it, carry):
        trip(nfull - 1 - it * per_trip, per_trip)
        return carry

    lax.fori_loop(0, i // SB_TRIP_BLOCKS, body, 0)
    for rem in range(1, SB_TRIP_BLOCKS):
        @pl.when(i % SB_TRIP_BLOCKS == rem)
        def _():
            trip(rem * ndiag - 1, rem * ndiag)
    o_ref[0] = acc_ref[...]


def _sb_prompt(q, k, v, bias):
    bsz, t, _ = q.shape
    tq, tk = SB_TQ, SB_TK
    t_idx = np.arange(tk)
    ntri = jnp.asarray(-(t_idx[:, None] > t_idx[None, :]).astype(np.float32), BF16)
    grid_spec = pltpu.PrefetchScalarGridSpec(
        num_scalar_prefetch=1,
        grid=(bsz, SB_HEADS // 2, t // tq),
        in_specs=[pl.BlockSpec((1, tq, LANES), lambda b, hp, i, bias: (b, i, hp)),
                  pl.BlockSpec((1, t, LANES), lambda b, hp, i, bias: (b, 0, hp)),
                  pl.BlockSpec((1, t, LANES), lambda b, hp, i, bias: (b, 0, hp)),
                  pl.BlockSpec((tk, tk), lambda b, hp, i, bias: (0, 0))],
        out_specs=pl.BlockSpec((1, tq, LANES), lambda b, hp, i, bias: (b, i, hp)),
        scratch_shapes=[pltpu.VMEM((tq, LANES), F32)] * 3,
    )
    return pl.pallas_call(
        functools.partial(_sb_prompt_kernel, tq=tq, tk=tk),
        grid_spec=grid_spec,
        out_shape=jax.ShapeDtypeStruct((bsz, t, SB_W), F32),
        compiler_params=_cparams(("parallel", "parallel", "arbitrary")),
        name="sb_prompt",
    )(bias, q, k, v, ntri)


def _sb_sample_kernel(pt_ref, q_ref, bias_ref, kn_ref, vn_ref, *rest, n_new, page, pages_per_step):
    kc_refs = rest[:pages_per_step]
    vc_refs = rest[pages_per_step:2 * pages_per_step]
    ntri_ref, o_ref, acc_ref, car_ref = rest[2 * pages_per_step:]
    j = pl.program_id(1)
    rows = SB_HEADS * SUBLANES
    q = q_ref[0]
    bias = bias_ref[...]
    ntri = ntri_ref[...]

    def run(k_refs, v_refs, mask):
        n = len(k_refs)
        kt = jnp.concatenate([r[0].reshape(SB_W, page).astype(BF16) for r in k_refs], axis=1)
        vt = jnp.concatenate([r[0].reshape(SB_W, page).astype(BF16) for r in v_refs], axis=1)
        z_all = _dot(q, kt)
        zs, sps = [], []
        for r in range(n):
            z = z_all[:, r * page:(r + 1) * page] + bias
            sp = _softplus2(z)
            if mask is not None:
                sp = jnp.where(mask, sp, 0.0)
            zs.append(z)
            sps.append(sp)
        between = _dot(jnp.concatenate(sps, axis=0).astype(BF16), ntri)
        car = car_ref[...]
        a_list = []
        for r in range(n):
            a = jnp.exp2((zs[r] - sps[r]) + between[r * rows:(r + 1) * rows] + car)
            if mask is not None:
                a = jnp.where(mask, a, 0.0)
            a_list.append(a.astype(BF16))
            car = car - jnp.broadcast_to(jnp.sum(sps[r], axis=-1, keepdims=True), car.shape)
        acc_ref[...] += _dot_nt(jnp.concatenate(a_list, axis=1), vt)
        car_ref[...] = car

    @pl.when(j == 0)
    def _():
        acc_ref[...] = jnp.zeros_like(acc_ref)
        car_ref[...] = jnp.zeros_like(car_ref)
        t_i = lax.broadcasted_iota(jnp.int32, (rows, page), 0) % SUBLANES
        s_i = lax.broadcasted_iota(jnp.int32, (rows, page), 1)
        run([kn_ref], [vn_ref], (s_i < t_i) & (s_i < n_new))

    run(kc_refs, vc_refs, None)

    @pl.when(j == pl.num_programs(1) - 1)
    def _():
        acc = acc_ref[...]
        for h in range(SB_HEADS):
            o_ref[0, h] = acc[h * SUBLANES:(h + 1) * SUBLANES, h * SB_DIM:(h + 1) * SB_DIM]


def _sb_sample(q_bd, bias_rows, kt_new, vt_new, cache_kt, cache_vt, page_table, n_new):
    bsz = q_bd.shape[0]
    page = cache_kt.shape[3]
    n_pages = page_table.shape[1]
    pps = max(p for p in range(1, SB_SAMPLE_PAGES + 1) if n_pages % p == 0)
    rows = SB_HEADS * SUBLANES
    t_idx = np.arange(page)
    ntri = jnp.asarray(-(t_idx[:, None] > t_idx[None, :]).astype(np.float32), BF16)

    def cache_map(r):
        return lambda b, j, pt: (pt[b * n_pages + (n_pages - 1 - (j * pps + r))], 0, 0, 0)

    per_b3 = lambda b, j, pt: (b, 0, 0)
    per_b4 = lambda b, j, pt: (b, 0, 0, 0)
    page_block = (1, SB_HEADS, SB_DIM, page)
    grid_spec = pltpu.PrefetchScalarGridSpec(
        num_scalar_prefetch=1,
        grid=(bsz, n_pages // pps),
        in_specs=[pl.BlockSpec((1, rows, SB_W), per_b3),
                  pl.BlockSpec((rows, page), lambda b, j, pt: (0, 0)),
                  pl.BlockSpec(page_block, per_b4),
                  pl.BlockSpec(page_block, per_b4)]
                 + [pl.BlockSpec(page_block, cache_map(r)) for r in list(range(pps)) * 2]
                 + [pl.BlockSpec((page, page), lambda b, j, pt: (0, 0))],
        out_specs=pl.BlockSpec((1, SB_HEADS, SUBLANES, SB_DIM), per_b4),
        scratch_shapes=[pltpu.VMEM((rows, SB_W), F32), pltpu.VMEM((rows, page), F32)],
    )
    return pl.pallas_call(
        functools.partial(_sb_sample_kernel, n_new=n_new, page=page, pages_per_step=pps),
        grid_spec=grid_spec,
        out_shape=jax.ShapeDtypeStruct((bsz, SB_HEADS, SUBLANES, SB_DIM), F32),
        compiler_params=_cparams(("parallel", "arbitrary")),
        name="sb_sample",
    )(page_table.reshape(-1), q_bd, bias_rows, kt_new, vt_new, *([cache_kt] * pps), *([cache_vt] * pps), ntri)


def _mix_out_kernel(h_ref, og_ref, gr_ref, os_ref, ggain_ref, sgain_ref, wo_ref, fgain_ref, rw_ref, rb_ref,
                    h1_ref, xn_ref, eidx_ref, gate_ref):
    og = og_ref[...]
    parts = []
    for hh in range(GLA_HEADS):
        x = og[:, hh * GLA_DV:(hh + 1) * GLA_DV]
        parts.append(x * lax.rsqrt(jnp.mean(x * x, axis=-1, keepdims=True) + EPS))
    gr = gr_ref[...]
    og_n = (jnp.concatenate(parts, axis=1) * ggain_ref[...]) * (gr * (1.0 / (1.0 + jnp.exp(-gr))))
    os_n = _half_lane_rms(os_ref[...], sgain_ref[...])
    h1 = h_ref[...] + (_dot(og_n.astype(BF16), wo_ref[:GLA_V_W, :]) + _dot(os_n.astype(BF16), wo_ref[GLA_V_W:, :]))
    h1_ref[...] = h1

    xn = (h1 * lax.rsqrt(jnp.mean(h1 * h1, axis=-1, keepdims=True) + EPS)) * fgain_ref[...]
    for j in range(xn.shape[1] // LANES):
        xn_ref[pl.ds(j, xn.shape[0], stride=SLAB), :] = xn[:, j * LANES:(j + 1) * LANES]

    x_hi, x_lo = _split_bf16(xn)
    rw = rw_ref[...]
    l2 = _dot(x_hi, rw) + _dot(x_lo, rw)
    logits = l2[:, :LANES] + l2[:, LANES:] + rb_ref[...]
    lane = _lane_iota(logits.shape).astype(F32)
    big = jnp.float32(4 * LANES)
    neg = jnp.float32(-jnp.inf)

    is_g = (lane >= N_EXPERTS) & (lane < N_EXPERTS + N_GROUPS)
    lg = jnp.where(is_g, logits, neg)
    mg = jnp.max(lg, axis=-1, keepdims=True)
    g_val = 1.0 / jnp.sum(jnp.exp(lg - mg), axis=-1, keepdims=True)
    g_idx = jnp.min(jnp.where(lg == mg, lane, big), axis=-1, keepdims=True) - N_EXPERTS

    in_g = (lane >= g_idx * EXPERTS_PER_GROUP) & (lane < (g_idx + 1) * EXPERTS_PER_GROUP)
    le = jnp.where(in_g, logits, neg)
    m1 = jnp.max(le, axis=-1, keepdims=True)
    se = jnp.sum(jnp.exp(le - m1), axis=-1, keepdims=True)
    i1 = jnp.min(jnp.where(le == m1, lane, big), axis=-1, keepdims=True)
    le2 = jnp.where(lane == i1, neg, le)
    m2 = jnp.max(le2, axis=-1, keepdims=True)
    i2 = jnp.min(jnp.where(le2 == m2, lane, big), axis=-1, keepdims=True)
    p1 = 1.0 / se
    p2 = jnp.exp(m2 - m1) / se
    tot = p1 + p2
    w1 = g_val * p1 / tot
    w2 = g_val * p2 / tot
    eidx_ref[...] = jnp.where(lane == 0, i1, jnp.where(lane == 1, i2, 0.0))[:, :SUBLANES].astype(jnp.int32)
    gate_ref[...] = jnp.where(lane == 0, w1, jnp.where(lane == 1, w2, 0.0))[:, :SUBLANES]


def _mix_out(h2, og, gr, osb, ggain, sgain, w_out_b, fgain, rw, rb):
    n, d = h2.shape
    tm = min(ROW_TILE, n)
    row = lambda w: pl.BlockSpec((tm, w), lambda i: (i, 0))
    full = lambda a: pl.BlockSpec(a.shape, lambda i: (0,) * a.ndim)
    return pl.pallas_call(
        _mix_out_kernel,
        grid=(n // tm,),
        in_specs=[row(d), row(GLA_V_W), row(GLA_V_W), row(SB_W), full(ggain), full(sgain), full(w_out_b),
                  full(fgain), full(rw), full(rb)],
        out_specs=[row(d), pl.BlockSpec((tm * SLAB, LANES), lambda i: (i, 0)), row(SUBLANES),
                   row(SUBLANES)],
        out_shape=[jax.ShapeDtypeStruct((n, d), F32), jax.ShapeDtypeStruct((n * SLAB, LANES), F32),
                   jax.ShapeDtypeStruct((n, SUBLANES), jnp.int32), jax.ShapeDtypeStruct((n, SUBLANES), F32)],
        compiler_params=_cparams(("parallel",)),
        name="mix_out",
    )(h2, og, gr, osb, ggain, sgain, w_out_b, fgain, rw, rb)


def _moe_slots_kernel(eidx_ref, ltri_ref, dest_ref, cnt_ref, counts_ref, run_ref, pstart_ref):
    ph = pl.program_id(0)
    i = pl.program_id(1)
    e = eidx_ref[...]
    tm = e.shape[0]
    lane = _lane_iota((tm, LANES))
    oh0 = (lane == e[:, 0:1]).astype(F32)
    oh1 = (lane == e[:, 1:2]).astype(F32)
    tot0 = jnp.sum(oh0, axis=0, keepdims=True)
    tot1 = jnp.sum(oh1, axis=0, keepdims=True)

    @pl.when((ph == 0) & (i == 0))
    def _():
        counts_ref[...] = jnp.zeros_like(counts_ref)

    @pl.when(ph == 0)
    def _():
        counts_ref[...] += tot0 + tot1

    @pl.when((ph == 1) & (i == 0))
    def _():
        cnt = counts_ref[...]
        padded = jnp.floor((cnt + (MOE_BLOCK - 1)) * (1.0 / MOE_BLOCK)) * MOE_BLOCK
        x = jnp.broadcast_to(padded, (SUBLANES, LANES))
        l8 = _lane_iota((SUBLANES, LANES))
        s = 1
        while s < LANES:
            x = x + jnp.where(l8 >= s, pltpu.roll(x, s, axis=1), 0.0)
            s *= 2
        pstart_ref[...] = x[0:1] - padded
        run_ref[...] = jnp.zeros_like(run_ref)
        cnt_ref[...] = jnp.broadcast_to(cnt, (SUBLANES, LANES)).astype(jnp.int32)

    @pl.when(ph == 1)
    def _():
        ltri = ltri_ref[...]
        base0 = run_ref[...] + pstart_ref[...]
        c0 = _dot(ltri, oh0.astype(BF16))
        c1 = _dot(ltri, oh1.astype(BF16))
        d0 = jnp.sum(oh0 * (base0 + c0), axis=-1, keepdims=True)
        d1 = jnp.sum(oh1 * (base0 + tot0 + c1), axis=-1, keepdims=True)
        l8 = _lane_iota((tm, LANES))
        dest_ref[...] = jnp.where(l8 == 0, d0, jnp.where(l8 == 1, d1, 0.0))[:, :SUBLANES].astype(jnp.int32)
        run_ref[...] += tot0 + tot1


def _moe_slots(eidx):
    n = eidx.shape[0]
    tm = MOE_TILE
    t_idx = np.arange(tm)
    ltri = jnp.asarray((t_idx[None, :] < t_idx[:, None]).astype(np.float32), BF16)
    return pl.pallas_call(
        _moe_slots_kernel,
        grid=(2, n // tm),
        in_specs=[pl.BlockSpec((tm, SUBLANES), lambda ph, i: (i, 0)),
                  pl.BlockSpec((tm, tm), lambda ph, i: (0, 0))],
        out_specs=[pl.BlockSpec((tm, SUBLANES), lambda ph, i: (i * ph, 0)),
                   pl.BlockSpec((SUBLANES, LANES), lambda ph, i: (0, 0))],
        out_shape=[jax.ShapeDtypeStruct((n, SUBLANES), jnp.int32),
                   jax.ShapeDtypeStruct((SUBLANES, LANES), jnp.int32)],
        scratch_shapes=[pltpu.VMEM((1, LANES), F32)] * 3,
        compiler_params=_cparams(("arbitrary", "arbitrary")),
        name="moe_slots",
    )(eidx, ltri)


def _load_slots(d0_hbm, d1_hbm, d0_s, d1_s, isem, tile, tm):
    base = pl.multiple_of(tile * tm, tm)
    c0 = pltpu.make_async_copy(d0_hbm.at[pl.ds(base, tm)], d0_s, isem.at[0])
    c1 = pltpu.make_async_copy(d1_hbm.at[pl.ds(base, tm)], d1_s, isem.at[1])
    c0.start()
    c1.start()
    c0.wait()
    c1.wait()


def _on_parity(i, fn):
    for s in (0, 1):
        pl.when(i % 2 == s)(functools.partial(fn, s))


def _moe_dispatch_kernel(cnt_ref, end_ref, d0_hbm, d1_hbm, x_ref, xs_hbm, d0_s, d1_s, zero_ref, isem, zsem, sem,
                         *, tm):
    i = pl.program_id(0)

    @pl.when(i == 0)
    def _():
        zero_ref[...] = jnp.zeros_like(zero_ref)

        def clear(e):
            first = pl.multiple_of((end_ref[e] - MOE_BLOCK) * SLAB, SLAB)
            return pltpu.make_async_copy(zero_ref, xs_hbm.at[pl.ds(first, MOE_BLOCK * SLAB)], zsem)

        for e in range(N_EXPERTS):
            pl.when(cnt_ref[e] > 0)(lambda e=e: clear(e).start())
        for e in range(N_EXPERTS):
            pl.when(cnt_ref[e] > 0)(lambda e=e: clear(e).wait())

        used = end_ref[N_EXPERTS - 1] // MOE_BLOCK
        total = xs_hbm.shape[0] // (MOE_BLOCK * SLAB)

        def spare(b):
            first = pl.multiple_of(b * (MOE_BLOCK * SLAB), SLAB)
            return pltpu.make_async_copy(zero_ref, xs_hbm.at[pl.ds(first, MOE_BLOCK * SLAB)], zsem)

        lax.fori_loop(used, total, lambda b, c: (spare(b).start(), c)[1], 0)
        lax.fori_loop(used, total, lambda b, c: (spare(b).wait(), c)[1], 0)

    _load_slots(d0_hbm, d1_hbm, d0_s, d1_s, isem, i, tm)

    def copies(r):
        src = x_ref.at[_slab(r)]
        return (pltpu.make_async_copy(src, xs_hbm.at[_slab(d0_s[r])], sem.at[0]),
                pltpu.make_async_copy(src, xs_hbm.at[_slab(d1_s[r])], sem.at[1]))

    def issue(r, carry):
        a, b = copies(r)
        a.start()
        b.start()
        return carry

    def drain(r, carry):
        a, b = copies(r)
        a.wait()
        b.wait()
        return carry

    lax.fori_loop(0, tm, issue, 0, unroll=DMA_LOOP_UNROLL)
    lax.fori_loop(0, tm, drain, 0, unroll=DMA_LOOP_UNROLL)


def _moe_dispatch(counts, end_rows, d0, d1, xn_slab, m_pad):
    n = xn_slab.shape[0] // SLAB
    tm = MOE_TILE
    grid_spec = pltpu.PrefetchScalarGridSpec(
        num_scalar_prefetch=2,
        grid=(n // tm,),
        in_specs=[pl.BlockSpec(memory_space=pl.ANY), pl.BlockSpec(memory_space=pl.ANY),
                  pl.BlockSpec((tm * SLAB, LANES), lambda i, cnt, end: (i, 0))],
        out_specs=pl.BlockSpec(memory_space=pl.ANY),
        scratch_shapes=[pltpu.SMEM((tm,), jnp.int32), pltpu.SMEM((tm,), jnp.int32),
                        pltpu.VMEM((MOE_BLOCK * SLAB, LANES), F32),
                        pltpu.SemaphoreType.DMA((2,)), pltpu.SemaphoreType.DMA(()),
                        pltpu.SemaphoreType.DMA((2,))],
    )
    return pl.pallas_call(
        functools.partial(_moe_dispatch_kernel, tm=tm),
        grid_spec=grid_spec,
        out_shape=jax.ShapeDtypeStruct((m_pad * SLAB, LANES), F32),
        compiler_params=_cparams(("arbitrary",)),
        name="moe_dispatch",
    )(counts, end_rows, d0, d1, xn_slab)


def _moe_experts_kernel(be_ref, nu_ref, xs_ref, wg_ref, wu_ref, wd_ref, y_ref, wgb_ref, wub_ref, wdb_ref):
    i = pl.program_id(0)
    rows = xs_ref.shape[0] // SLAB

    @pl.when((i == 0) | (be_ref[i] != be_ref[jnp.maximum(i - 1, 0)]))
    def _():
        wgb_ref[...] = wg_ref[0].astype(BF16)
        wub_ref[...] = wu_ref[0].astype(BF16)
        wdb_ref[...] = wd_ref[0].astype(BF16)

    @pl.when(i < nu_ref[0])
    def _():
        x = jnp.concatenate([xs_ref[pl.ds(j, rows, stride=SLAB), :] for j in range(SLAB)], axis=1).astype(BF16)
        g = _dot(x, wgb_ref[...])
        u = _dot(x, wub_ref[...])
        hdn = (g * (1.0 / (1.0 + jnp.exp(-g)))) * u
        y = _dot(hdn.astype(BF16), wdb_ref[...])
        for j in range(SLAB):
            y_ref[pl.ds(j, rows, stride=SLAB), :] = y[:, j * LANES:(j + 1) * LANES]

    @pl.when(i >= nu_ref[0])
    def _():
        y_ref[...] = jnp.zeros_like(y_ref)


def _moe_experts(block_e, n_used, xs, wg, wu, wd):
    m_pad = xs.shape[0] // SLAB
    nb = m_pad // MOE_BLOCK
    d, de = wg.shape[1], wg.shape[2]
    rows = pl.BlockSpec((MOE_BLOCK * SLAB, LANES), lambda i, be, nu: (i, 0))
    grid_spec = pltpu.PrefetchScalarGridSpec(
        num_scalar_prefetch=2,
        grid=(nb,),
        in_specs=[pl.BlockSpec((MOE_BLOCK * SLAB, LANES), lambda i, be, nu: (jnp.minimum(i, nu[0] - 1), 0)),
                  pl.BlockSpec((1, d, de), lambda i, be, nu: (be[i], 0, 0)),
                  pl.BlockSpec((1, d, de), lambda i, be, nu: (be[i], 0, 0)),
                  pl.BlockSpec((1, de, d), lambda i, be, nu: (be[i], 0, 0))],
        out_specs=rows,
        scratch_shapes=[pltpu.VMEM((d, de), BF16), pltpu.VMEM((d, de), BF16), pltpu.VMEM((de, d), BF16)],
    )
    return pl.pallas_call(
        _moe_experts_kernel,
        grid_spec=grid_spec,
        out_shape=jax.ShapeDtypeStruct((m_pad * SLAB, LANES), F32),
        compiler_params=_cparams(("arbitrary",)),
        name="moe_experts",
    )(block_e, n_used, xs, wg, wu, wd)


def _moe_combine_kernel(d0_hbm, d1_hbm, gate_ref, h1_ref, y_hbm, out_ref, d0_a, d1_a, d0_b, d1_b,
                        buf0_a, buf1_a, buf0_b, buf1_b, isem, sem, *, tm):
    i = pl.program_id(0)
    last = pl.num_programs(0) - 1
    idx = ((d0_a, d1_a), (d0_b, d1_b))
    bufs = ((buf0_a, buf1_a), (buf0_b, buf1_b))
    cr = MOE_COMBINE_ROWS

    def copies(s, r):
        return (pltpu.make_async_copy(y_hbm.at[_slab(idx[s][0][r])], bufs[s][0].at[_slab(r)], sem.at[s, 0]),
                pltpu.make_async_copy(y_hbm.at[_slab(idx[s][1][r])], bufs[s][1].at[_slab(r)], sem.at[s, 1]))

    def gather(tile, s):
        _load_slots(d0_hbm, d1_hbm, idx[s][0], idx[s][1], isem, tile, tm)

        def body(r, carry):
            a, b = copies(s, r)
            a.start()
            b.start()
            return carry
        lax.fori_loop(0, tm, body, 0, unroll=DMA_LOOP_UNROLL)

    def run(s):
        @pl.when(i == 0)
        def _():
            gather(i, s)

        @pl.when(i < last)
        def _():
            gather(i + 1, 1 - s)

        def drain(r, carry):
            a, b = copies(s, r)
            a.wait()
            b.wait()
            return carry
        lax.fori_loop(0, tm, drain, 0, unroll=DMA_LOOP_UNROLL)

        def chunk(c, carry):
            r0 = pl.multiple_of(c * cr, cr)
            rows = pl.ds(r0, cr)
            w0 = jnp.broadcast_to(gate_ref[rows, 0:1], (cr, LANES))
            w1 = jnp.broadcast_to(gate_ref[rows, 1:2], (cr, LANES))
            for j in range(SLAB):
                srows = pl.ds(r0 * SLAB + j, cr, stride=SLAB)
                y = bufs[s][0][srows, :] * w0 + bufs[s][1][srows, :] * w1
                out_ref[rows, j * LANES:(j + 1) * LANES] = h1_ref[rows, j * LANES:(j + 1) * LANES] + y
            return carry
        lax.fori_loop(0, tm // cr, chunk, 0)

    _on_parity(i, run)


def _moe_combine(d0, d1, gate, h1, y_rows):
    n, d = h1.shape
    tm = MOE_TILE
    return pl.pallas_call(
        functools.partial(_moe_combine_kernel, tm=tm),
        grid=(n // tm,),
        in_specs=[pl.BlockSpec(memory_space=pl.ANY), pl.BlockSpec(memory_space=pl.ANY),
                  pl.BlockSpec((tm, SUBLANES), lambda i: (i, 0)),
                  pl.BlockSpec((tm, d), lambda i: (i, 0)),
                  pl.BlockSpec(memory_space=pl.ANY)],
        out_specs=pl.BlockSpec((tm, d), lambda i: (i, 0)),
        out_shape=jax.ShapeDtypeStruct((n, d), F32),
        scratch_shapes=[pltpu.SMEM((tm,), jnp.int32)] * 4 + [pltpu.VMEM((tm * SLAB, LANES), F32)] * 4
                       + [pltpu.SemaphoreType.DMA((2,)), pltpu.SemaphoreType.DMA((2, 2))],
        compiler_params=_cparams(("arbitrary",)),
        name="moe_combine",
    )(d0, d1, gate, h1, y_rows)


def _moe(h1, xn_slab, eidx, gate, wg, wu, wd):
    n = h1.shape[0]
    assert h1.shape[1] == SLAB * LANES
    n_pad = -(-n // MOE_TILE) * MOE_TILE
    if n_pad != n:
        padr = lambda a, k: jnp.pad(a, ((0, k * (n_pad - n)),) + ((0, 0),) * (a.ndim - 1))
        h1, xn_slab, eidx, gate = padr(h1, 1), padr(xn_slab, SLAB), padr(eidx, 1), padr(gate, 1)
    dest, cnt = _moe_slots(eidx)
    d0, d1 = dest[:, 0], dest[:, 1]
    nb = (2 * n_pad) // MOE_BLOCK + N_EXPERTS
    counts = cnt[0, :N_EXPERTS]
    pend = jnp.cumsum((counts + MOE_BLOCK - 1) // MOE_BLOCK)
    block_e = jnp.minimum(jnp.sum(pend[None, :] <= jnp.arange(nb, dtype=jnp.int32)[:, None], axis=1),
                          N_EXPERTS - 1).astype(jnp.int32)
    n_used = pend[-1:].astype(jnp.int32)
    xs = _moe_dispatch(counts, (pend * MOE_BLOCK).astype(jnp.int32), d0, d1, xn_slab, nb * MOE_BLOCK)
    y_rows = _moe_experts(block_e, n_used, xs, wg, wu, wd)
    return _moe_combine(d0, d1, gate, h1, y_rows)[:n]


def kernel(x_prompt, x_sample, cache_sb_k, cache_sb_v, state_gla, page_table, meta_tokens, norm_mix_gain, w_in, gla_w_alpha, gla_b_alpha, gla_out_gain, sb_q_gain, sb_k_gain, sb_logit_bias, sb_out_gain, w_out, norm_ffn_gain, router_group, router_group_b, router_expert, router_expert_b, w_gate, w_up, w_down):
    bsz, seq, d = x_prompt.shape
    dbs, dseq, _ = x_sample.shape
    depth = w_in.shape[0]
    page = cache_sb_k.shape[2]
    t_real = seq + N_META
    t_pad = -(-t_real // SB_TQ) * SB_TQ
    fpad = t_pad - t_real

    hp = jnp.concatenate([jnp.zeros((bsz, fpad, d), x_prompt.dtype),
                          jnp.broadcast_to(meta_tokens[None].astype(x_prompt.dtype), (bsz, N_META, d)),
                          x_prompt], axis=1).reshape(bsz * t_pad, d)
    hs = x_sample.reshape(dbs * dseq, d)

    outs = {k: [] for k in ("kp", "vp", "sp", "ks", "vs", "ss")}
    for l in range(depth):
        w = w_in[l]
        w_in_r = jnp.concatenate([w[:, :1536], w[:, 1552:3088], w[:, 1536:1552],
                                  jnp.zeros((d, _W_IN_COLS - 3088), w.dtype)], axis=1).astype(BF16)
        wa_pad = jnp.pad(gla_w_alpha[l], ((0, LANES - GLA_RANK), (0, 0)))
        ba = gla_b_alpha[l][None]
        norm_g = norm_mix_gain[l][None]
        qgain = jnp.tile(sb_q_gain[l], SB_HEADS)[None]
        kgain = jnp.tile(sb_k_gain[l], SB_HEADS)[None]
        ggain = jnp.tile(gla_out_gain[l], GLA_HEADS)[None]
        sgain = jnp.tile(sb_out_gain[l], SB_HEADS)[None]
        fgain = norm_ffn_gain[l][None]
        w_out_b = w_out[l].astype(BF16)
        r_all = jnp.concatenate([router_expert[l].transpose(1, 0, 2).reshape(d, N_EXPERTS), router_group[l],
                                 jnp.zeros((d, LANES - N_EXPERTS - N_GROUPS), F32)], axis=1)
        r_hi = r_all.astype(BF16)
        r_lo = (r_all - r_hi.astype(F32)).astype(BF16)
        rw = jnp.concatenate([r_hi, r_lo], axis=1)
        rb = jnp.concatenate([router_expert_b[l].reshape(N_EXPERTS), router_group_b[l],
                              jnp.zeros((LANES - N_EXPERTS - N_GROUPS,), F32)])[None]
        wg, wu, wd = w_gate[l], w_up[l], w_down[l]
        bias2 = sb_logit_bias[l].astype(F32) * LOG2E
        b_hi = bias2.astype(BF16).astype(F32)
        b_lo = (bias2 - b_hi).astype(BF16).astype(F32)
        bias_hl = jnp.stack([b_hi, b_lo], axis=1).reshape(-1)

        gq, gk, gv, gr, la, qs, ks, vs, ksb, vsb = _in_proj(hp, norm_g, w_in_r, wa_pad, ba, qgain, kgain)
        b3 = lambda a: a.reshape(bsz, t_pad, a.shape[-1])
        o_g, s_p = _gla(b3(gq), b3(gk), b3(gv), b3(la), jnp.zeros((bsz, GLA_QK_W, GLA_DV), F32), fpad)
        o_s = _sb_prompt(b3(qs), b3(ksb), b3(vsb), bias_hl)
        h1, xn3, eidx, gate = _mix_out(hp, o_g.reshape(-1, GLA_V_W), gr, o_s.reshape(-1, SB_W), ggain, sgain,
                                       w_out_b, fgain, rw, rb)
        hp = _moe(h1, xn3, eidx, gate, wg, wu, wd)
        outs["kp"].append(b3(ks)[:, fpad:].reshape(bsz, t_real, SB_HEADS, SB_DIM))
        outs["vp"].append(b3(vs)[:, fpad:].reshape(bsz, t_real, SB_HEADS, SB_DIM))
        outs["sp"].append(s_p.reshape(bsz, GLA_HEADS, GLA_DK, GLA_DV))

        gq, gk, gv, gr, la, qs, ks, vs, ksb, vsb = _in_proj(hs, norm_g, w_in_r, wa_pad, ba, qgain, kgain)
        cpad = GLA_CHUNK - dseq
        c3 = lambda a: jnp.pad(a.reshape(dbs, dseq, a.shape[-1]), ((0, 0), (cpad, 0), (0, 0)))
        o_g, s_s = _gla(c3(gq), c3(gk), c3(gv), c3(la), state_gla[l].reshape(dbs, GLA_QK_W, GLA_DV), cpad)
        o_g = o_g[:, cpad:].reshape(dbs * dseq, GLA_V_W)
        q4 = qs.reshape(dbs, dseq, SB_HEADS, SB_DIM).transpose(0, 2, 1, 3)
        q4 = jnp.pad(q4, ((0, 0), (0, 0), (0, SUBLANES - dseq), (0, 0)))
        eye = jnp.eye(SB_HEADS, dtype=q4.dtype)
        q_bd = (q4[:, :, :, None, :] * eye[None, :, None, :, None]).reshape(dbs, SB_HEADS * SUBLANES, SB_W)
        bias_rows = jnp.broadcast_to(jnp.repeat(bias2, SUBLANES)[:, None], (SB_HEADS * SUBLANES, page))
        to_t = lambda a: jnp.pad(a.reshape(dbs, dseq, SB_HEADS, SB_DIM).transpose(0, 2, 3, 1),
                                 ((0, 0), (0, 0), (0, 0), (0, page - dseq)))
        cache_kt = cache_sb_k[l].transpose(0, 2, 3, 1)
        cache_vt = cache_sb_v[l].transpose(0, 2, 3, 1)
        o4 = _sb_sample(q_bd, bias_rows, to_t(ks), to_t(vs), cache_kt, cache_vt, page_table, dseq)
        o_s = o4[:, :, :dseq].transpose(0, 2, 1, 3).reshape(dbs * dseq, SB_W)
        h1, xn3, eidx, gate = _mix_out(hs, o_g, gr, o_s, ggain, sgain, w_out_b, fgain, rw, rb)
        hs = _moe(h1, xn3, eidx, gate, wg, wu, wd)
        outs["ks"].append(ks.reshape(dbs, dseq, SB_HEADS, SB_DIM))
        outs["vs"].append(vs.reshape(dbs, dseq, SB_HEADS, SB_DIM))
        outs["ss"].append(s_s.reshape(dbs, GLA_HEADS, GLA_DK, GLA_DV))

    y_prompt = hp.reshape(bsz, t_pad, d)[:, fpad + N_META:]
    y_sample = hs.reshape(dbs, dseq, d)
    return (y_prompt, y_sample, jnp.stack(outs["kp"]), jnp.stack(outs["vp"]), jnp.stack(outs["sp"]),
            jnp.stack(outs["ks"]), jnp.stack(outs["vs"]), jnp.stack(outs["ss"]))
```

```python
import functools

import jax
import jax.numpy as jnp
import numpy as np
from jax import lax
from jax.experimental import pallas as pl
from jax.experimental.pallas import tpu as pltpu

F32 = jnp.float32
BF16 = jnp.bfloat16

N_META = 16
GLA_HEADS = 4
GLA_DK = 64
GLA_DV = 128
GLA_RANK = 16
GLA_TAU = 16.0
GLA_QK_W = GLA_HEADS * GLA_DK
GLA_V_W = GLA_HEADS * GLA_DV
SB_HEADS = 8
SB_DIM = 64
SB_W = SB_HEADS * SB_DIM
N_GROUPS = 4
EXPERTS_PER_GROUP = 8
N_EXPERTS = N_GROUPS * EXPERTS_PER_GROUP
EPS = 1e-6
LOG2E = 1.4426950408889634

LANES = 128
SUBLANES = 8
VMEM_LIMIT_BYTES = 56 * 1024 * 1024

ROW_TILE = 512
GLA_CHUNK = 128
GLA_BATCH = 4
SB_TQ = 768
SB_TK = 256
SB_TRIP_BLOCKS = 2
SB_SAMPLE_PAGES = 32
MOE_TILE = 1024
MOE_BLOCK = 512
MOE_COMBINE_ROWS = 128
DMA_LOOP_UNROLL = 8
SLAB = 8

_C_GQ, _C_GK, _C_GV, _C_GR, _C_SQ, _C_SK, _C_SV, _C_GA = 0, 256, 512, 1024, 1536, 2048, 2560, 3072
_W_IN_COLS = 3200


def _cparams(sem):
    return pltpu.CompilerParams(dimension_semantics=sem, vmem_limit_bytes=VMEM_LIMIT_BYTES)


def _dot(a, b):
    return jnp.dot(a, b, preferred_element_type=F32)


def _dot_nt(a, b):
    return lax.dot_general(a, b, (((1,), (1,)), ((), ())), preferred_element_type=F32)


def _split_bf16(x):
    hi = x.astype(BF16)
    lo = (x - hi.astype(F32)).astype(BF16)
    return hi, lo


def _slab(r):
    return pl.ds(pl.multiple_of(r * SLAB, SLAB), SLAB)


def _lane_iota(shape):
    return lax.broadcasted_iota(jnp.int32, shape, len(shape) - 1)


def _half_lane_rms(x, gain):
    outs = []
    for g in range(x.shape[1] // LANES):
        xg = x[:, g * LANES:(g + 1) * LANES]
        x2 = xg * xg
        low = _lane_iota(xg.shape) < SB_DIM
        s_all = jnp.sum(x2, axis=-1, keepdims=True)
        s_lo = jnp.sum(jnp.where(low, x2, 0.0), axis=-1, keepdims=True)
        ms = jnp.where(low, s_lo, s_all - s_lo) * (1.0 / SB_DIM)
        outs.append(xg * lax.rsqrt(ms + EPS))
    return jnp.concatenate(outs, axis=1) * gain


def _in_proj_kernel(x_ref, g_ref, w_ref, wa_ref, ba_ref, qgain_ref, kgain_ref,
                    gq_ref, gk_ref, gv_ref, gr_ref, la_ref, qs_ref, ks_ref, vs_ref, ksb_ref, vsb_ref):
    x = x_ref[...]
    ms = jnp.mean(x * x, axis=-1, keepdims=True)
    xn = ((x * lax.rsqrt(ms + EPS)) * g_ref[...]).astype(BF16)

    def proj(c0, width):
        return _dot(xn, w_ref[:, c0:c0 + width])

    gq_ref[...] = proj(_C_GQ, GLA_QK_W) * (GLA_DK ** -0.5)
    gk_ref[...] = proj(_C_GK, GLA_QK_W)
    gv_ref[...] = proj(_C_GV, GLA_V_W)
    gr_ref[...] = proj(_C_GR, GLA_V_W)

    ga_hi, ga_lo = _split_bf16(proj(_C_GA, LANES))
    wa_hi, wa_lo = _split_bf16(wa_ref[...])
    u = _dot(ga_hi, wa_hi) + _dot(ga_lo, wa_hi) + _dot(ga_hi, wa_lo) + ba_ref[...]
    la_ref[...] = (jnp.minimum(u, 0.0) - jnp.log(1.0 + jnp.exp(-jnp.abs(u)))) * (1.0 / GLA_TAU)

    q_s = _half_lane_rms(proj(_C_SQ, SB_W), qgain_ref[...])
    qs_ref[...] = (q_s * (SB_DIM ** -0.5 * LOG2E)).astype(BF16)
    k_s = _half_lane_rms(proj(_C_SK, SB_W), kgain_ref[...])
    ks_ref[...] = k_s
    ksb_ref[...] = k_s.astype(BF16)
    v_s = proj(_C_SV, SB_W)
    vs_ref[...] = v_s
    vsb_ref[...] = v_s.astype(BF16)


def _in_proj(h2, norm_g, w_in_r, wa_pad, ba, qgain, kgain):
    n, d = h2.shape
    tm = min(ROW_TILE, n)
    row = lambda w: pl.BlockSpec((tm, w), lambda i: (i, 0))
    full = lambda a: pl.BlockSpec(a.shape, lambda i: (0,) * a.ndim)
    outs = [(GLA_QK_W, F32), (GLA_QK_W, F32), (GLA_V_W, F32), (GLA_V_W, F32), (GLA_QK_W, F32),
            (SB_W, BF16), (SB_W, F32), (SB_W, F32), (SB_W, BF16), (SB_W, BF16)]
    return pl.pallas_call(
        _in_proj_kernel,
        grid=(n // tm,),
        in_specs=[row(d), full(norm_g), full(w_in_r), full(wa_pad), full(ba), full(qgain), full(kgain)],
        out_specs=[row(w) for w, _ in outs],
        out_shape=[jax.ShapeDtypeStruct((n, w), dt) for w, dt in outs],
        compiler_params=_cparams(("parallel",)),
        name="in_proj",
    )(h2, norm_g, w_in_r, wa_pad, ba, qgain, kgain)


def _gla_levels(c):
    levels = []
    l = c // 2
    while l >= 1:
        levels.append(l)
        l //= 2
    return levels


def _gla_constants(c):
    t = np.arange(c)
    tri = (t[None, :] <= t[:, None]).astype(np.float32)
    mats, masks = [tri], []
    for l in _gla_levels(c):
        mid = (t // (2 * l)) * (2 * l) + l
        if l < SUBLANES:
            mats.append((t[None, :] <= (mid[:, None] - 1)).astype(np.float32))
        same = (t[:, None] // (2 * l)) == (t[None, :] // (2 * l))
        masks.append((same & ((t[:, None] % (2 * l)) >= l) & ((t[None, :] % (2 * l)) < l)).astype(np.float32))
    masks.append(np.eye(c, dtype=np.float32))
    return np.concatenate(mats, axis=0), np.stack(masks)


def _gla_kernel(q_ref, k_ref, v_ref, la_ref, s0_ref, gmat_ref, mask_ref, o_ref, s_out_ref, st_ref,
                *, chunk, front_pad):
    for bb in range(q_ref.shape[0]):
        _gla_chunk(bb, q_ref, k_ref, v_ref, la_ref, s0_ref, gmat_ref, mask_ref, o_ref, s_out_ref, st_ref,
                   chunk, front_pad)


def _gla_chunk(bb, q_ref, k_ref, v_ref, la_ref, s0_ref, gmat_ref, mask_ref, o_ref, s_out_ref, st_ref,
               chunk, front_pad):
    c = pl.program_id(1)
    levels = _gla_levels(chunk)
    w = GLA_QK_W

    @pl.when(c == 0)
    def _():
        st_ref[bb] = s0_ref[bb].T

    q = q_ref[bb]
    k = k_ref[bb]
    la = la_ref[bb]
    if front_pad:
        row = lax.broadcasted_iota(jnp.int32, la.shape, 0) + c * chunk
        la = jnp.where(row < front_pad, 0.0, la)
    la_hi, la_lo = _split_bf16(la)
    p = _dot(gmat_ref[...], jnp.concatenate([la_hi, la_lo], axis=1))
    p = p[:, :w] + p[:, w:]
    b = p[:chunk]
    lane = _lane_iota((1, w))
    head_masks = [(lane >= h * GLA_DK) & (lane < (h + 1) * GLA_DK) for h in range(GLA_HEADS)]

    scores = [jnp.zeros((chunk, chunk), F32) for _ in range(GLA_HEADS)]
    n_mat = 0
    for i, l in enumerate(levels):
        if l < SUBLANES:
            n_mat += 1
            r = p[n_mat * chunk:(n_mat + 1) * chunk]
        else:
            r = jnp.concatenate([jnp.broadcast_to(b[g + l - 1:g + l], (2 * l, w))
                                 for g in range(0, chunk, 2 * l)], axis=0)
        qt = q * jnp.exp(jnp.minimum(b - r, 0.0))
        kt = (k * jnp.exp(jnp.minimum(r - b, 0.0))).astype(BF16)
        for h in range(GLA_HEADS):
            qh = jnp.where(head_masks[h], qt, 0.0).astype(BF16)
            scores[h] = scores[h] + mask_ref[i] * _dot_nt(qh, kt)
    kb = k.astype(BF16)
    for h in range(GLA_HEADS):
        qh = jnp.where(head_masks[h], q, 0.0).astype(BF16)
        scores[h] = scores[h] + mask_ref[len(levels)] * _dot_nt(qh, kb)

    st = st_ref[bb]
    st_b = st.astype(BF16)
    b_last = b[chunk - 1:chunk]
    q_in = q * jnp.exp(b)
    k_out = k * jnp.exp(b_last - b)
    upd = jnp.zeros_like(st)
    for h in range(GLA_HEADS):
        vh = v_ref[bb, :, h * GLA_DV:(h + 1) * GLA_DV]
        vhb = vh.astype(BF16)
        qh = jnp.where(head_masks[h], q_in, 0.0).astype(BF16)
        o_ref[bb, :, h * GLA_DV:(h + 1) * GLA_DV] = _dot(scores[h].astype(BF16), vhb) + _dot_nt(qh, st_b)
        kh = jnp.where(head_masks[h], k_out, 0.0).astype(BF16)
        upd = upd + _dot(vh.T.astype(BF16), kh)
    st_new = st * jnp.exp(b_last) + upd
    st_ref[bb] = st_new

    @pl.when(c == pl.num_programs(1) - 1)
    def _():
        s_out_ref[bb] = st_new.T


def _gla(q, k, v, la, s0, front_pad):
    bsz, t, _ = q.shape
    chunk = GLA_CHUNK
    gmat, masks = _gla_constants(chunk)
    gmat = jnp.asarray(gmat, BF16)
    masks = jnp.asarray(masks, F32)
    nb = max(d for d in range(1, GLA_BATCH + 1) if bsz % d == 0)
    tok = lambda w: pl.BlockSpec((nb, chunk, w), lambda b, c: (b, c, 0))
    per_b = pl.BlockSpec((nb, GLA_QK_W, GLA_DV), lambda b, c: (b, 0, 0))
    return pl.pallas_call(
        functools.partial(_gla_kernel, chunk=chunk, front_pad=front_pad),
        grid=(bsz // nb, t // chunk),
        in_specs=[tok(GLA_QK_W), tok(GLA_QK_W), tok(GLA_V_W), tok(GLA_QK_W), per_b,
                  pl.BlockSpec(gmat.shape, lambda b, c: (0, 0)),
                  pl.BlockSpec(masks.shape, lambda b, c: (0, 0, 0))],
        out_specs=[tok(GLA_V_W), per_b],
        out_shape=[jax.ShapeDtypeStruct((bsz, t, GLA_V_W), F32),
                   jax.ShapeDtypeStruct((bsz, GLA_QK_W, GLA_DV), F32)],
        scratch_shapes=[pltpu.VMEM((nb, GLA_DV, GLA_QK_W), F32)],
        compiler_params=_cparams(("parallel", "arbitrary")),
        name="gla",
    )(q, k, v, la, s0, gmat, masks)


def _softplus2(z):
    neg_abs = lax.bitcast_convert_type(lax.bitcast_convert_type(z, jnp.uint32) | jnp.uint32(0x80000000), F32)
    return jnp.maximum(z, 0.0) + jnp.log2(1.0 + jnp.exp2(neg_abs))


def _sb_prompt_kernel(bias_ref, q_ref, k_ref, v_ref, ntri_ref, o_ref, acc_ref, c0_ref, c1_ref, *, tq, tk):
    hp = pl.program_id(1)
    i = pl.program_id(2)
    ndiag = tq // tk
    q = q_ref[0].astype(F32)
    lane_q = _lane_iota(q.shape)
    q0 = jnp.where(lane_q < SB_DIM, q, jnp.where(lane_q < SB_DIM + 2, 1.0, 0.0)).astype(BF16)
    q1 = jnp.where(lane_q >= SB_DIM, q, jnp.where(lane_q < 2, 1.0, 0.0)).astype(BF16)
    ntri = ntri_ref[...]
    lane_k = _lane_iota((tk, LANES))
    low_k = lane_k < SB_DIM

    def bias_lanes(h, first_lane):
        hi = jnp.full((tk, LANES), bias_ref[4 * hp + 2 * h], F32)
        lo = jnp.full((tk, LANES), bias_ref[4 * hp + 2 * h + 1], F32)
        return jnp.where(lane_k == first_lane, hi, jnp.where(lane_k == first_lane + 1, lo, 0.0)).astype(BF16)

    kbias0 = bias_lanes(0, SB_DIM)
    kbias1 = bias_lanes(1, 0)

    def head(qh, kh, vh, carry, mask):
        z = _dot_nt(qh, kh)
        sp = _softplus2(z)
        if mask is not None:
            sp = jnp.where(mask, sp, 0.0)
        between = _dot(sp.astype(BF16), ntri)
        a = jnp.exp2((z - sp) + between + jnp.concatenate([carry] * (tk // LANES), axis=1))
        if mask is not None:
            a = jnp.where(mask, a, 0.0)
        rowsum = jnp.broadcast_to(jnp.sum(sp, axis=-1, keepdims=True), carry.shape)
        return _dot(a.astype(BF16), vh), carry - rowsum

    def tile(kt, r0, mask, c0, c1):
        rows = pl.ds(pl.multiple_of(kt * tk, tk), tk)
        kb = k_ref[0, rows, :]
        vb = v_ref[0, rows, :]
        zk = jnp.zeros_like(vb)
        o0, c0 = head(q0[r0:], jnp.where(low_k, kb, kbias0), jnp.where(low_k, vb, zk), c0, mask)
        o1, c1 = head(q1[r0:], jnp.where(low_k, kbias1, kb), jnp.where(low_k, zk, vb), c1, mask)
        return o0 + o1, c0, c1

    nfull = i * ndiag

    acc = jnp.zeros((tq, LANES), F32)
    c0 = jnp.zeros((tq, LANES), F32)
    c1 = jnp.zeros((tq, LANES), F32)
    for d in reversed(range(ndiag)):
        r0 = d * tk
        nr = tq - r0
        mask = lax.broadcasted_iota(jnp.int32, (nr, tk), 0) > lax.broadcasted_iota(jnp.int32, (nr, tk), 1)
        o, c0n, c1n = tile(nfull + d, r0, mask, c0[r0:], c1[r0:])
        tail_rows = lambda full, new: new if r0 == 0 else jnp.concatenate([full[:r0], new], axis=0)
        acc = tail_rows(acc, acc[r0:] + o)
        c0 = tail_rows(c0, c0n)
        c1 = tail_rows(c1, c1n)
    acc_ref[...] = acc
    c0_ref[...] = c0
    c1_ref[...] = c1

    def trip(first_kt, ntiles):
        c0 = c0_ref[...]
        c1 = c1_ref[...]
        total = None
        for u in range(ntiles):
            o, c0, c1 = tile(first_kt - u, 0, None, c0, c1)
            total = o if total is None else total + o
        acc_ref[...] += total
        c0_ref[...] = c0
        c1_ref[...] = c1

    per_trip = SB_TRIP_BLOCKS * ndiag

    def body(it, carry):
        trip(nfull - 1 - it * per_trip, per_trip)
        return carry

    lax.fori_loop(0, i // SB_TRIP_BLOCKS, body, 0)
    for rem in range(1, SB_TRIP_BLOCKS):
        @pl.when(i % SB_TRIP_BLOCKS == rem)
        def _():
            trip(rem * ndiag - 1, rem * ndiag)
    o_ref[0] = acc_ref[...]


def _sb_prompt(q, k, v, bias):
    bsz, t, _ = q.shape
    tq, tk = SB_TQ, SB_TK
    t_idx = np.arange(tk)
    ntri = jnp.asarray(-(t_idx[:, None] > t_idx[None, :]).astype(np.float32), BF16)
    grid_spec = pltpu.PrefetchScalarGridSpec(
        num_scalar_prefetch=1,
        grid=(bsz, SB_HEADS // 2, t // tq),
        in_specs=[pl.BlockSpec((1, tq, LANES), lambda b, hp, i, bias: (b, i, hp)),
                  pl.BlockSpec((1, t, LANES), lambda b, hp, i, bias: (b, 0, hp)),
                  pl.BlockSpec((1, t, LANES), lambda b, hp, i, bias: (b, 0, hp)),
                  pl.BlockSpec((tk, tk), lambda b, hp, i, bias: (0, 0))],
        out_specs=pl.BlockSpec((1, tq, LANES), lambda b, hp, i, bias: (b, i, hp)),
        scratch_shapes=[pltpu.VMEM((tq, LANES), F32)] * 3,
    )
    return pl.pallas_call(
        functools.partial(_sb_prompt_kernel, tq=tq, tk=tk),
        grid_spec=grid_spec,
        out_shape=jax.ShapeDtypeStruct((bsz, t, SB_W), F32),
        compiler_params=_cparams(("parallel", "parallel", "arbitrary")),
        name="sb_prompt",
    )(bias, q, k, v, ntri)


def _sb_sample_kernel(pt_ref, q_ref, bias_ref, kn_ref, vn_ref, *rest, n_new, page, pages_per_step):
    kc_refs = rest[:pages_per_step]
    vc_refs = rest[pages_per_step:2 * pages_per_step]
    ntri_ref, o_ref, acc_ref, car_ref = rest[2 * pages_per_step:]
    j = pl.program_id(1)
    rows = SB_HEADS * SUBLANES
    q = q_ref[0]
    bias = bias_ref[...]
    ntri = ntri_ref[...]

    def run(k_refs, v_refs, mask):
        n = len(k_refs)
        kt = jnp.concatenate([r[0].reshape(SB_W, page).astype(BF16) for r in k_refs], axis=1)
        vt = jnp.concatenate([r[0].reshape(SB_W, page).astype(BF16) for r in v_refs], axis=1)
        z_all = _dot(q, kt)
        zs, sps = [], []
        for r in range(n):
            z = z_all[:, r * page:(r + 1) * page] + bias
            sp = _softplus2(z)
            if mask is not None:
                sp = jnp.where(mask, sp, 0.0)
            zs.append(z)
            sps.append(sp)
        between = _dot(jnp.concatenate(sps, axis=0).astype(BF16), ntri)
        car = car_ref[...]
        a_list = []
        for r in range(n):
            a = jnp.exp2((zs[r] - sps[r]) + between[r * rows:(r + 1) * rows] + car)
            if mask is not None:
                a = jnp.where(mask, a, 0.0)
            a_list.append(a.astype(BF16))
            car = car - jnp.broadcast_to(jnp.sum(sps[r], axis=-1, keepdims=True), car.shape)
        acc_ref[...] += _dot_nt(jnp.concatenate(a_list, axis=1), vt)
        car_ref[...] = car

    @pl.when(j == 0)
    def _():
        acc_ref[...] = jnp.zeros_like(acc_ref)
        car_ref[...] = jnp.zeros_like(car_ref)
        t_i = lax.broadcasted_iota(jnp.int32, (rows, page), 0) % SUBLANES
        s_i = lax.broadcasted_iota(jnp.int32, (rows, page), 1)
        run([kn_ref], [vn_ref], (s_i < t_i) & (s_i < n_new))

    run(kc_refs, vc_refs, None)

    @pl.when(j == pl.num_programs(1) - 1)
    def _():
        acc = acc_ref[...]
        for h in range(SB_HEADS):
            o_ref[0, h] = acc[h * SUBLANES:(h + 1) * SUBLANES, h * SB_DIM:(h + 1) * SB_DIM]


def _sb_sample(q_bd, bias_rows, kt_new, vt_new, cache_kt, cache_vt, page_table, n_new):
    bsz = q_bd.shape[0]
    page = cache_kt.shape[3]
    n_pages = page_table.shape[1]
    pps = max(p for p in range(1, SB_SAMPLE_PAGES + 1) if n_pages % p == 0)
    rows = SB_HEADS * SUBLANES
    t_idx = np.arange(page)
    ntri = jnp.asarray(-(t_idx[:, None] > t_idx[None, :]).astype(np.float32), BF16)

    def cache_map(r):
        return lambda b, j, pt: (pt[b * n_pages + (n_pages - 1 - (j * pps + r))], 0, 0, 0)

    per_b3 = lambda b, j, pt: (b, 0, 0)
    per_b4 = lambda b, j, pt: (b, 0, 0, 0)
    page_block = (1, SB_HEADS, SB_DIM, page)
    grid_spec = pltpu.PrefetchScalarGridSpec(
        num_scalar_prefetch=1,
        grid=(bsz, n_pages // pps),
        in_specs=[pl.BlockSpec((1, rows, SB_W), per_b3),
                  pl.BlockSpec((rows, page), lambda b, j, pt: (0, 0)),
                  pl.BlockSpec(page_block, per_b4),
                  pl.BlockSpec(page_block, per_b4)]
                 + [pl.BlockSpec(page_block, cache_map(r)) for r in list(range(pps)) * 2]
                 + [pl.BlockSpec((page, page), lambda b, j, pt: (0, 0))],
        out_specs=pl.BlockSpec((1, SB_HEADS, SUBLANES, SB_DIM), per_b4),
        scratch_shapes=[pltpu.VMEM((rows, SB_W), F32), pltpu.VMEM((rows, page), F32)],
    )
    return pl.pallas_call(
        functools.partial(_sb_sample_kernel, n_new=n_new, page=page, pages_per_step=pps),
        grid_spec=grid_spec,
        out_shape=jax.ShapeDtypeStruct((bsz, SB_HEADS, SUBLANES, SB_DIM), F32),
        compiler_params=_cparams(("parallel", "arbitrary")),
        name="sb_sample",
    )(page_table.reshape(-1), q_bd, bias_rows, kt_new, vt_new, *([cache_kt] * pps), *([cache_vt] * pps), ntri)


def _mix_out_kernel(h_ref, og_ref, gr_ref, os_ref, ggain_ref, sgain_ref, wo_ref, fgain_ref, rw_ref, rb_ref,
                    h1_ref, xn_ref, eidx_ref, gate_ref):
    og = og_ref[...]
    parts = []
    for hh in range(GLA_HEADS):
        x = og[:, hh * GLA_DV:(hh + 1) * GLA_DV]
        parts.append(x * lax.rsqrt(jnp.mean(x * x, axis=-1, keepdims=True) + EPS))
    gr = gr_ref[...]
    og_n = (jnp.concatenate(parts, axis=1) * ggain_ref[...]) * (gr * (1.0 / (1.0 + jnp.exp(-gr))))
    os_n = _half_lane_rms(os_ref[...], sgain_ref[...])
    h1 = h_ref[...] + (_dot(og_n.astype(BF16), wo_ref[:GLA_V_W, :]) + _dot(os_n.astype(BF16), wo_ref[GLA_V_W:, :]))
    h1_ref[...] = h1

    xn = (h1 * lax.rsqrt(jnp.mean(h1 * h1, axis=-1, keepdims=True) + EPS)) * fgain_ref[...]
    for j in range(xn.shape[1] // LANES):
        xn_ref[pl.ds(j, xn.shape[0], stride=SLAB), :] = xn[:, j * LANES:(j + 1) * LANES]

    x_hi, x_lo = _split_bf16(xn)
    rw = rw_ref[...]
    l2 = _dot(x_hi, rw) + _dot(x_lo, rw)
    logits = l2[:, :LANES] + l2[:, LANES:] + rb_ref[...]
    lane = _lane_iota(logits.shape).astype(F32)
    big = jnp.float32(4 * LANES)
    neg = jnp.float32(-jnp.inf)

    is_g = (lane >= N_EXPERTS) & (lane < N_EXPERTS + N_GROUPS)
    lg = jnp.where(is_g, logits, neg)
    mg = jnp.max(lg, axis=-1, keepdims=True)
    g_val = 1.0 / jnp.sum(jnp.exp(lg - mg), axis=-1, keepdims=True)
    g_idx = jnp.min(jnp.where(lg == mg, lane, big), axis=-1, keepdims=True) - N_EXPERTS

    in_g = (lane >= g_idx * EXPERTS_PER_GROUP) & (lane < (g_idx + 1) * EXPERTS_PER_GROUP)
    le = jnp.where(in_g, logits, neg)
    m1 = jnp.max(le, axis=-1, keepdims=True)
    se = jnp.sum(jnp.exp(le - m1), axis=-1, keepdims=True)
    i1 = jnp.min(jnp.where(le == m1, lane, big), axis=-1, keepdims=True)
    le2 = jnp.where(lane == i1, neg, le)
    m2 = jnp.max(le2, axis=-1, keepdims=True)
    i2 = jnp.min(jnp.where(le2 == m2, lane, big), axis=-1, keepdims=True)
    p1 = 1.0 / se
    p2 = jnp.exp(m2 - m1) / se
    tot = p1 + p2
    w1 = g_val * p1 / tot
    w2 = g_val * p2 / tot
    eidx_ref[...] = jnp.where(lane == 0, i1, jnp.where(lane == 1, i2, 0.0))[:, :SUBLANES].astype(jnp.int32)
    gate_ref[...] = jnp.where(lane == 0, w1, jnp.where(lane == 1, w2, 0.0))[:, :SUBLANES]


def _mix_out(h2, og, gr, osb, ggain, sgain, w_out_b, fgain, rw, rb):
    n, d = h2.shape
    tm = min(ROW_TILE, n)
    row = lambda w: pl.BlockSpec((tm, w), lambda i: (i, 0))
    full = lambda a: pl.BlockSpec(a.shape, lambda i: (0,) * a.ndim)
    return pl.pallas_call(
        _mix_out_kernel,
        grid=(n // tm,),
        in_specs=[row(d), row(GLA_V_W), row(GLA_V_W), row(SB_W), full(ggain), full(sgain), full(w_out_b),
                  full(fgain), full(rw), full(rb)],
        out_specs=[row(d), pl.BlockSpec((tm * SLAB, LANES), lambda i: (i, 0)), row(SUBLANES),
                   row(SUBLANES)],
        out_shape=[jax.ShapeDtypeStruct((n, d), F32), jax.ShapeDtypeStruct((n * SLAB, LANES), F32),
                   jax.ShapeDtypeStruct((n, SUBLANES), jnp.int32), jax.ShapeDtypeStruct((n, SUBLANES), F32)],
        compiler_params=_cparams(("parallel",)),
        name="mix_out",
    )(h2, og, gr, osb, ggain, sgain, w_out_b, fgain, rw, rb)


def _moe_slots_kernel(eidx_ref, ltri_ref, dest_ref, cnt_ref, counts_ref, run_ref, pstart_ref):
    ph = pl.program_id(0)
    i = pl.program_id(1)
    e = eidx_ref[...]
    tm = e.shape[0]
    lane = _lane_iota((tm, LANES))
    oh0 = (lane == e[:, 0:1]).astype(F32)
    oh1 = (lane == e[:, 1:2]).astype(F32)
    tot0 = jnp.sum(oh0, axis=0, keepdims=True)
    tot1 = jnp.sum(oh1, axis=0, keepdims=True)

    @pl.when((ph == 0) & (i == 0))
    def _():
        counts_ref[...] = jnp.zeros_like(counts_ref)

    @pl.when(ph == 0)
    def _():
        counts_ref[...] += tot0 + tot1

    @pl.when((ph == 1) & (i == 0))
    def _():
        cnt = counts_ref[...]
        padded = jnp.floor((cnt + (MOE_BLOCK - 1)) * (1.0 / MOE_BLOCK)) * MOE_BLOCK
        x = jnp.broadcast_to(padded, (SUBLANES, LANES))
        l8 = _lane_iota((SUBLANES, LANES))
        s = 1
        while s < LANES:
            x = x + jnp.where(l8 >= s, pltpu.roll(x, s, axis=1), 0.0)
            s *= 2
        pstart_ref[...] = x[0:1] - padded
        run_ref[...] = jnp.zeros_like(run_ref)
        cnt_ref[...] = jnp.broadcast_to(cnt, (SUBLANES, LANES)).astype(jnp.int32)

    @pl.when(ph == 1)
    def _():
        ltri = ltri_ref[...]
        base0 = run_ref[...] + pstart_ref[...]
        c0 = _dot(ltri, oh0.astype(BF16))
        c1 = _dot(ltri, oh1.astype(BF16))
        d0 = jnp.sum(oh0 * (base0 + c0), axis=-1, keepdims=True)
        d1 = jnp.sum(oh1 * (base0 + tot0 + c1), axis=-1, keepdims=True)
        l8 = _lane_iota((tm, LANES))
        dest_ref[...] = jnp.where(l8 == 0, d0, jnp.where(l8 == 1, d1, 0.0))[:, :SUBLANES].astype(jnp.int32)
        run_ref[...] += tot0 + tot1


def _moe_slots(eidx):
    n = eidx.shape[0]
    tm = MOE_TILE
    t_idx = np.arange(tm)
    ltri = jnp.asarray((t_idx[None, :] < t_idx[:, None]).astype(np.float32), BF16)
    return pl.pallas_call(
        _moe_slots_kernel,
        grid=(2, n // tm),
        in_specs=[pl.BlockSpec((tm, SUBLANES), lambda ph, i: (i, 0)),
                  pl.BlockSpec((tm, tm), lambda ph, i: (0, 0))],
        out_specs=[pl.BlockSpec((tm, SUBLANES), lambda ph, i: (i * ph, 0)),
                   pl.BlockSpec((SUBLANES, LANES), lambda ph, i: (0, 0))],
        out_shape=[jax.ShapeDtypeStruct((n, SUBLANES), jnp.int32),
                   jax.ShapeDtypeStruct((SUBLANES, LANES), jnp.int32)],
        scratch_shapes=[pltpu.VMEM((1, LANES), F32)] * 3,
        compiler_params=_cparams(("arbitrary", "arbitrary")),
        name="moe_slots",
    )(eidx, ltri)


def _load_slots(d0_hbm, d1_hbm, d0_s, d1_s, isem, tile, tm):
    base = pl.multiple_of(tile * tm, tm)
    c0 = pltpu.make_async_copy(d0_hbm.at[pl.ds(base, tm)], d0_s, isem.at[0])
    c1 = pltpu.make_async_copy(d1_hbm.at[pl.ds(base, tm)], d1_s, isem.at[1])
    c0.start()
    c1.start()
    c0.wait()
    c1.wait()


def _on_parity(i, fn):
    for s in (0, 1):
        pl.when(i % 2 == s)(functools.partial(fn, s))


def _moe_dispatch_kernel(cnt_ref, end_ref, d0_hbm, d1_hbm, x_ref, xs_hbm, d0_s, d1_s, zero_ref, isem, zsem, sem,
                         *, tm):
    i = pl.program_id(0)

    @pl.when(i == 0)
    def _():
        zero_ref[...] = jnp.zeros_like(zero_ref)

        def clear(e):
            first = pl.multiple_of((end_ref[e] - MOE_BLOCK) * SLAB, SLAB)
            return pltpu.make_async_copy(zero_ref, xs_hbm.at[pl.ds(first, MOE_BLOCK * SLAB)], zsem)

        for e in range(N_EXPERTS):
            pl.when(cnt_ref[e] > 0)(lambda e=e: clear(e).start())
        for e in range(N_EXPERTS):
            pl.when(cnt_ref[e] > 0)(lambda e=e: clear(e).wait())

        used = end_ref[N_EXPERTS - 1] // MOE_BLOCK
        total = xs_hbm.shape[0] // (MOE_BLOCK * SLAB)

        def spare(b):
            first = pl.multiple_of(b * (MOE_BLOCK * SLAB), SLAB)
            return pltpu.make_async_copy(zero_ref, xs_hbm.at[pl.ds(first, MOE_BLOCK * SLAB)], zsem)

        lax.fori_loop(used, total, lambda b, c: (spare(b).start(), c)[1], 0)
        lax.fori_loop(used, total, lambda b, c: (spare(b).wait(), c)[1], 0)

    _load_slots(d0_hbm, d1_hbm, d0_s, d1_s, isem, i, tm)

    def copies(r):
        src = x_ref.at[_slab(r)]
        return (pltpu.make_async_copy(src, xs_hbm.at[_slab(d0_s[r])], sem.at[0]),
                pltpu.make_async_copy(src, xs_hbm.at[_slab(d1_s[r])], sem.at[1]))

    def issue(r, carry):
        a, b = copies(r)
        a.start()
        b.start()
        return carry

    def drain(r, carry):
        a, b = copies(r)
        a.wait()
        b.wait()
        return carry

    lax.fori_loop(0, tm, issue, 0, unroll=DMA_LOOP_UNROLL)
    lax.fori_loop(0, tm, drain, 0, unroll=DMA_LOOP_UNROLL)


def _moe_dispatch(counts, end_rows, d0, d1, xn_slab, m_pad):
    n = xn_slab.shape[0] // SLAB
    tm = MOE_TILE
    grid_spec = pltpu.PrefetchScalarGridSpec(
        num_scalar_prefetch=2,
        grid=(n // tm,),
        in_specs=[pl.BlockSpec(memory_space=pl.ANY), pl.BlockSpec(memory_space=pl.ANY),
                  pl.BlockSpec((tm * SLAB, LANES), lambda i, cnt, end: (i, 0))],
        out_specs=pl.BlockSpec(memory_space=pl.ANY),
        scratch_shapes=[pltpu.SMEM((tm,), jnp.int32), pltpu.SMEM((tm,), jnp.int32),
                        pltpu.VMEM((MOE_BLOCK * SLAB, LANES), F32),
                        pltpu.SemaphoreType.DMA((2,)), pltpu.SemaphoreType.DMA(()),
                        pltpu.SemaphoreType.DMA((2,))],
    )
    return pl.pallas_call(
        functools.partial(_moe_dispatch_kernel, tm=tm),
        grid_spec=grid_spec,
        out_shape=jax.ShapeDtypeStruct((m_pad * SLAB, LANES), F32),
        compiler_params=_cparams(("arbitrary",)),
        name="moe_dispatch",
    )(counts, end_rows, d0, d1, xn_slab)


def _moe_experts_kernel(be_ref, nu_ref, xs_ref, wg_ref, wu_ref, wd_ref, y_ref, wgb_ref, wub_ref, wdb_ref):
    i = pl.program_id(0)
    rows = xs_ref.shape[0] // SLAB

    @pl.when((i == 0) | (be_ref[i] != be_ref[jnp.maximum(i - 1, 0)]))
    def _():
        wgb_ref[...] = wg_ref[0].astype(BF16)
        wub_ref[...] = wu_ref[0].astype(BF16)
        wdb_ref[...] = wd_ref[0].astype(BF16)

    @pl.when(i < nu_ref[0])
    def _():
        x = jnp.concatenate([xs_ref[pl.ds(j, rows, stride=SLAB), :] for j in range(SLAB)], axis=1).astype(BF16)
        g = _dot(x, wgb_ref[...])
        u = _dot(x, wub_ref[...])
        hdn = (g * (1.0 / (1.0 + jnp.exp(-g)))) * u
        y = _dot(hdn.astype(BF16), wdb_ref[...])
        for j in range(SLAB):
            y_ref[pl.ds(j, rows, stride=SLAB), :] = y[:, j * LANES:(j + 1) * LANES]

    @pl.when(i >= nu_ref[0])
    def _():
        y_ref[...] = jnp.zeros_like(y_ref)


def _moe_experts(block_e, n_used, xs, wg, wu, wd):
    m_pad = xs.shape[0] // SLAB
    nb = m_pad // MOE_BLOCK
    d, de = wg.shape[1], wg.shape[2]
    rows = pl.BlockSpec((MOE_BLOCK * SLAB, LANES), lambda i, be, nu: (i, 0))
    grid_spec = pltpu.PrefetchScalarGridSpec(
        num_scalar_prefetch=2,
        grid=(nb,),
        in_specs=[pl.BlockSpec((MOE_BLOCK * SLAB, LANES), lambda i, be, nu: (jnp.minimum(i, nu[0] - 1), 0)),
                  pl.BlockSpec((1, d, de), lambda i, be, nu: (be[i], 0, 0)),
                  pl.BlockSpec((1, d, de), lambda i, be, nu: (be[i], 0, 0)),
                  pl.BlockSpec((1, de, d), lambda i, be, nu: (be[i], 0, 0))],
        out_specs=rows,
        scratch_shapes=[pltpu.VMEM((d, de), BF16), pltpu.VMEM((d, de), BF16), pltpu.VMEM((de, d), BF16)],
    )
    return pl.pallas_call(
        _moe_experts_kernel,
        grid_spec=grid_spec,
        out_shape=jax.ShapeDtypeStruct((m_pad * SLAB, LANES), F32),
        compiler_params=_cparams(("arbitrary",)),
        name="moe_experts",
    )(block_e, n_used, xs, wg, wu, wd)


def _moe_combine_kernel(d0_hbm, d1_hbm, gate_ref, h1_ref, y_hbm, out_ref, d0_a, d1_a, d0_b, d1_b,
                        buf0_a, buf1_a, buf0_b, buf1_b, isem, sem, *, tm):
    i = pl.program_id(0)
    last = pl.num_programs(0) - 1
    idx = ((d0_a, d1_a), (d0_b, d1_b))
    bufs = ((buf0_a, buf1_a), (buf0_b, buf1_b))
    cr = MOE_COMBINE_ROWS

    def copies(s, r):
        return (pltpu.make_async_copy(y_hbm.at[_slab(idx[s][0][r])], bufs[s][0].at[_slab(r)], sem.at[s, 0]),
                pltpu.make_async_copy(y_hbm.at[_slab(idx[s][1][r])], bufs[s][1].at[_slab(r)], sem.at[s, 1]))

    def gather(tile, s):
        _load_slots(d0_hbm, d1_hbm, idx[s][0], idx[s][1], isem, tile, tm)

        def body(r, carry):
            a, b = copies(s, r)
            a.start()
            b.start()
            return carry
        lax.fori_loop(0, tm, body, 0, unroll=DMA_LOOP_UNROLL)

    def run(s):
        @pl.when(i == 0)
        def _():
            gather(i, s)

        @pl.when(i < last)
        def _():
            gather(i + 1, 1 - s)

        def drain(r, carry):
            a, b = copies(s, r)
            a.wait()
            b.wait()
            return carry
        lax.fori_loop(0, tm, drain, 0, unroll=DMA_LOOP_UNROLL)

        def chunk(c, carry):
            r0 = pl.multiple_of(c * cr, cr)
            rows = pl.ds(r0, cr)
            w0 = jnp.broadcast_to(gate_ref[rows, 0:1], (cr, LANES))
            w1 = jnp.broadcast_to(gate_ref[rows, 1:2], (cr, LANES))
            for j in range(SLAB):
                srows = pl.ds(r0 * SLAB + j, cr, stride=SLAB)
                y = bufs[s][0][srows, :] * w0 + bufs[s][1][srows, :] * w1
                out_ref[rows, j * LANES:(j + 1) * LANES] = h1_ref[rows, j * LANES:(j + 1) * LANES] + y
            return carry
        lax.fori_loop(0, tm // cr, chunk, 0)

    _on_parity(i, run)


def _moe_combine(d0, d1, gate, h1, y_rows):
    n, d = h1.shape
    tm = MOE_TILE
    return pl.pallas_call(
        functools.partial(_moe_combine_kernel, tm=tm),
        grid=(n // tm,),
        in_specs=[pl.BlockSpec(memory_space=pl.ANY), pl.BlockSpec(memory_space=pl.ANY),
                  pl.BlockSpec((tm, SUBLANES), lambda i: (i, 0)),
                  pl.BlockSpec((tm, d), lambda i: (i, 0)),
                  pl.BlockSpec(memory_space=pl.ANY)],
        out_specs=pl.BlockSpec((tm, d), lambda i: (i, 0)),
        out_shape=jax.ShapeDtypeStruct((n, d), F32),
        scratch_shapes=[pltpu.SMEM((tm,), jnp.int32)] * 4 + [pltpu.VMEM((tm * SLAB, LANES), F32)] * 4
                       + [pltpu.SemaphoreType.DMA((2,)), pltpu.SemaphoreType.DMA((2, 2))],
        compiler_params=_cparams(("arbitrary",)),
        name="moe_combine",
    )(d0, d1, gate, h1, y_rows)


def _moe(h1, xn_slab, eidx, gate, wg, wu, wd):
    n = h1.shape[0]
    assert h1.shape[1] == SLAB * LANES
    n_pad = -(-n // MOE_TILE) * MOE_TILE
    if n_pad != n:
        padr = lambda a, k: jnp.pad(a, ((0, k * (n_pad - n)),) + ((0, 0),) * (a.ndim - 1))
        h1, xn_slab, eidx, gate = padr(h1, 1), padr(xn_slab, SLAB), padr(eidx, 1), padr(gate, 1)
    dest, cnt = _moe_slots(eidx)
    d0, d1 = dest[:, 0], dest[:, 1]
    nb = (2 * n_pad) // MOE_BLOCK + N_EXPERTS
    counts = cnt[0, :N_EXPERTS]
    pend = jnp.cumsum((counts + MOE_BLOCK - 1) // MOE_BLOCK)
    block_e = jnp.minimum(jnp.sum(pend[None, :] <= jnp.arange(nb, dtype=jnp.int32)[:, None], axis=1),
                          N_EXPERTS - 1).astype(jnp.int32)
    n_used = pend[-1:].astype(jnp.int32)
    xs = _moe_dispatch(counts, (pend * MOE_BLOCK).astype(jnp.int32), d0, d1, xn_slab, nb * MOE_BLOCK)
    y_rows = _moe_experts(block_e, n_used, xs, wg, wu, wd)
    return _moe_combine(d0, d1, gate, h1, y_rows)[:n]


def kernel(x_prompt, x_sample, cache_sb_k, cache_sb_v, state_gla, page_table, meta_tokens, norm_mix_gain, w_in, gla_w_alpha, gla_b_alpha, gla_out_gain, sb_q_gain, sb_k_gain, sb_logit_bias, sb_out_gain, w_out, norm_ffn_gain, router_group, router_group_b, router_expert, router_expert_b, w_gate, w_up, w_down):
    bsz, seq, d = x_prompt.shape
    dbs, dseq, _ = x_sample.shape
    depth = w_in.shape[0]
    page = cache_sb_k.shape[2]
    t_real = seq + N_META
    t_pad = -(-t_real // SB_TQ) * SB_TQ
    fpad = t_pad - t_real

    hp = jnp.concatenate([jnp.zeros((bsz, fpad, d), x_prompt.dtype),
                          jnp.broadcast_to(meta_tokens[None].astype(x_prompt.dtype), (bsz, N_META, d)),
                          x_prompt], axis=1).reshape(bsz * t_pad, d)
    hs = x_sample.reshape(dbs * dseq, d)

    outs = {k: [] for k in ("kp", "vp", "sp", "ks", "vs", "ss")}
    for l in range(depth):
        w = w_in[l]
        w_in_r = jnp.concatenate([w[:, :1536], w[:, 1552:3088], w[:, 1536:1552],
                                  jnp.zeros((d, _W_IN_COLS - 3088), w.dtype)], axis=1).astype(BF16)
        wa_pad = jnp.pad(gla_w_alpha[l], ((0, LANES - GLA_RANK), (0, 0)))
        ba = gla_b_alpha[l][None]
        norm_g = norm_mix_gain[l][None]
        qgain = jnp.tile(sb_q_gain[l], SB_HEADS)[None]
        kgain = jnp.tile(sb_k_gain[l], SB_HEADS)[None]
        ggain = jnp.tile(gla_out_gain[l], GLA_HEADS)[None]
        sgain = jnp.tile(sb_out_gain[l], SB_HEADS)[None]
        fgain = norm_ffn_gain[l][None]
        w_out_b = w_out[l].astype(BF16)
        r_all = jnp.concatenate([router_expert[l].transpose(1, 0, 2).reshape(d, N_EXPERTS), router_group[l],
                                 jnp.zeros((d, LANES - N_EXPERTS - N_GROUPS), F32)], axis=1)
        r_hi = r_all.astype(BF16)
        r_lo = (r_all - r_hi.astype(F32)).astype(BF16)
        rw = jnp.concatenate([r_hi, r_lo], axis=1)
        rb = jnp.concatenate([router_expert_b[l].reshape(N_EXPERTS), router_group_b[l],
                              jnp.zeros((LANES - N_EXPERTS - N_GROUPS,), F32)])[None]
        wg, wu, wd = w_gate[l], w_up[l], w_down[l]
        bias2 = sb_logit_bias[l].astype(F32) * LOG2E
        b_hi = bias2.astype(BF16).astype(F32)
        b_lo = (bias2 - b_hi).astype(BF16).astype(F32)
        bias_hl = jnp.stack([b_hi, b_lo], axis=1).reshape(-1)

        gq, gk, gv, gr, la, qs, ks, vs, ksb, vsb = _in_proj(hp, norm_g, w_in_r, wa_pad, ba, qgain, kgain)
        b3 = lambda a: a.reshape(bsz, t_pad, a.shape[-1])
        o_g, s_p = _gla(b3(gq), b3(gk), b3(gv), b3(la), jnp.zeros((bsz, GLA_QK_W, GLA_DV), F32), fpad)
        o_s = _sb_prompt(b3(qs), b3(ksb), b3(vsb), bias_hl)
        h1, xn3, eidx, gate = _mix_out(hp, o_g.reshape(-1, GLA_V_W), gr, o_s.reshape(-1, SB_W), ggain, sgain,
                                       w_out_b, fgain, rw, rb)
        hp = _moe(h1, xn3, eidx, gate, wg, wu, wd)
        outs["kp"].append(b3(ks)[:, fpad:].reshape(bsz, t_real, SB_HEADS, SB_DIM))
        outs["vp"].append(b3(vs)[:, fpad:].reshape(bsz, t_real, SB_HEADS, SB_DIM))
        outs["sp"].append(s_p.reshape(bsz, GLA_HEADS, GLA_DK, GLA_DV))

        gq, gk, gv, gr, la, qs, ks, vs, ksb, vsb = _in_proj(hs, norm_g, w_in_r, wa_pad, ba, qgain, kgain)
        cpad = GLA_CHUNK - dseq
        c3 = lambda a: jnp.pad(a.reshape(dbs, dseq, a.shape[-1]), ((0, 0), (cpad, 0), (0, 0)))
        o_g, s_s = _gla(c3(gq), c3(gk), c3(gv), c3(la), state_gla[l].reshape(dbs, GLA_QK_W, GLA_DV), cpad)
        o_g = o_g[:, cpad:].reshape(dbs * dseq, GLA_V_W)
        q4 = qs.reshape(dbs, dseq, SB_HEADS, SB_DIM).transpose(0, 2, 1, 3)
        q4 = jnp.pad(q4, ((0, 0), (0, 0), (0, SUBLANES - dseq), (0, 0)))
        eye = jnp.eye(SB_HEADS, dtype=q4.dtype)
        q_bd = (q4[:, :, :, None, :] * eye[None, :, None, :, None]).reshape(dbs, SB_HEADS * SUBLANES, SB_W)
        bias_rows = jnp.broadcast_to(jnp.repeat(bias2, SUBLANES)[:, None], (SB_HEADS * SUBLANES, page))
        to_t = lambda a: jnp.pad(a.reshape(dbs, dseq, SB_HEADS, SB_DIM).transpose(0, 2, 3, 1),
                                 ((0, 0), (0, 0), (0, 0), (0, page - dseq)))
        cache_kt = cache_sb_k[l].transpose(0, 2, 3, 1)
        cache_vt = cache_sb_v[l].transpose(0, 2, 3, 1)
        o4 = _sb_sample(q_bd, bias_rows, to_t(ks), to_t(vs), cache_kt, cache_vt, page_table, dseq)
        o_s = o4[:, :, :dseq].transpose(0, 2, 1, 3).reshape(dbs * dseq, SB_W)
        h1, xn3, eidx, gate = _mix_out(hs, o_g, gr, o_s, ggain, sgain, w_out_b, fgain, rw, rb)
        hs = _moe(h1, xn3, eidx, gate, wg, wu, wd)
        outs["ks"].append(ks.reshape(dbs, dseq, SB_HEADS, SB_DIM))
        outs["vs"].append(vs.reshape(dbs, dseq, SB_HEADS, SB_DIM))
        outs["ss"].append(s_s.reshape(dbs, GLA_HEADS, GLA_DK, GLA_DV))

    y_prompt = hp.reshape(bsz, t_pad, d)[:, fpad + N_META:]
    y_sample = hs.reshape(dbs, dseq, d)
    return (y_prompt, y_sample, jnp.stack(outs["kp"]), jnp.stack(outs["vp"]), jnp.stack(outs["sp"]),
            jnp.stack(outs["ks"]), jnp.stack(outs["vs"]), jnp.stack(outs["ss"]))
```

```python
import functools

import jax
import jax.numpy as jnp
import numpy as np
from jax import lax
from jax.experimental import pallas as pl
from jax.experimental.pallas import tpu as pltpu

F32 = jnp.float32
BF16 = jnp.bfloat16

N_META = 16
GLA_HEADS = 4
GLA_DK = 64
GLA_DV = 128
GLA_RANK = 16
GLA_TAU = 16.0
GLA_QK_W = GLA_HEADS * GLA_DK
GLA_V_W = GLA_HEADS * GLA_DV
SB_HEADS = 8
SB_DIM = 64
SB_W = SB_HEADS * SB_DIM
N_GROUPS = 4
EXPERTS_PER_GROUP = 8
N_EXPERTS = N_GROUPS * EXPERTS_PER_GROUP
EPS = 1e-6
LOG2E = 1.4426950408889634

LANES = 128
SUBLANES = 8
VMEM_LIMIT_BYTES = 56 * 1024 * 1024

ROW_TILE = 512
GLA_CHUNK = 128
GLA_BATCH = 4
SB_TQ = 768
SB_TK = 256
SB_TRIP_BLOCKS = 2
SB_SAMPLE_PAGES = 32
MOE_TILE = 1024
MOE_BLOCK = 512
MOE_COMBINE_ROWS = 128
DMA_LOOP_UNROLL = 8
SLAB = 8

_C_GQ, _C_GK, _C_GV, _C_GR, _C_SQ, _C_SK, _C_SV, _C_GA = 0, 256, 512, 1024, 1536, 2048, 2560, 3072
_W_IN_COLS = 3200


def _cparams(sem):
    return pltpu.CompilerParams(dimension_semantics=sem, vmem_limit_bytes=VMEM_LIMIT_BYTES)


def _dot(a, b):
    return jnp.dot(a, b, preferred_element_type=F32)


def _dot_nt(a, b):
    return lax.dot_general(a, b, (((1,), (1,)), ((), ())), preferred_element_type=F32)


def _split_bf16(x):
    hi = x.astype(BF16)
    lo = (x - hi.astype(F32)).astype(BF16)
    return hi, lo


def _slab(r):
    return pl.ds(pl.multiple_of(r * SLAB, SLAB), SLAB)


def _lane_iota(shape):
    return lax.broadcasted_iota(jnp.int32, shape, len(shape) - 1)


def _half_lane_rms(x, gain):
    outs = []
    for g in range(x.shape[1] // LANES):
        xg = x[:, g * LANES:(g + 1) * LANES]
        x2 = xg * xg
        low = _lane_iota(xg.shape) < SB_DIM
        s_lo = jnp.sum(jnp.where(low, x2, 0.0), axis=-1, keepdims=True)
        s_hi = jnp.sum(jnp.where(low, 0.0, x2), axis=-1, keepdims=True)
        ms = jnp.where(low, s_lo, s_hi) * (1.0 / SB_DIM)
        outs.append(xg * lax.rsqrt(ms + EPS))
    return jnp.concatenate(outs, axis=1) * gain


def _in_proj_kernel(x_ref, g_ref, w_ref, wa_ref, ba_ref, qgain_ref, kgain_ref,
                    gq_ref, gk_ref, gv_ref, gr_ref, la_ref, qs_ref, ks_ref, vs_ref, ksb_ref, vsb_ref):
    x = x_ref[...]
    ms = jnp.mean(x * x, axis=-1, keepdims=True)
    xn = ((x * lax.rsqrt(ms + EPS)) * g_ref[...]).astype(BF16)

    def proj(c0, width):
        return _dot(xn, w_ref[:, c0:c0 + width])

    gq_ref[...] = proj(_C_GQ, GLA_QK_W) * (GLA_DK ** -0.5)
    gk_ref[...] = proj(_C_GK, GLA_QK_W)
    gv_ref[...] = proj(_C_GV, GLA_V_W)
    gr_ref[...] = proj(_C_GR, GLA_V_W)

    ga_hi, ga_lo = _split_bf16(proj(_C_GA, LANES))
    wa_hi, wa_lo = _split_bf16(wa_ref[...])
    u = _dot(ga_hi, wa_hi) + _dot(ga_lo, wa_hi) + _dot(ga_hi, wa_lo) + ba_ref[...]
    la_ref[...] = (jnp.minimum(u, 0.0) - jnp.log(1.0 + jnp.exp(-jnp.abs(u)))) * (1.0 / GLA_TAU)

    q_s = _half_lane_rms(proj(_C_SQ, SB_W), qgain_ref[...])
    qs_ref[...] = (q_s * (SB_DIM ** -0.5 * LOG2E)).astype(BF16)
    k_s = _half_lane_rms(proj(_C_SK, SB_W), kgain_ref[...])
    ks_ref[...] = k_s
    ksb_ref[...] = k_s.astype(BF16)
    v_s = proj(_C_SV, SB_W)
    vs_ref[...] = v_s
    vsb_ref[...] = v_s.astype(BF16)


def _in_proj(h2, norm_g, w_in_r, wa_pad, ba, qgain, kgain):
    n, d = h2.shape
    tm = min(ROW_TILE, n)
    row = lambda w: pl.BlockSpec((tm, w), lambda i: (i, 0))
    full = lambda a: pl.BlockSpec(a.shape, lambda i: (0,) * a.ndim)
    outs = [(GLA_QK_W, F32), (GLA_QK_W, F32), (GLA_V_W, F32), (GLA_V_W, F32), (GLA_QK_W, F32),
            (SB_W, BF16), (SB_W, F32), (SB_W, F32), (SB_W, BF16), (SB_W, BF16)]
    return pl.pallas_call(
        _in_proj_kernel,
        grid=(n // tm,),
        in_specs=[row(d), full(norm_g), full(w_in_r), full(wa_pad), full(ba), full(qgain), full(kgain)],
        out_specs=[row(w) for w, _ in outs],
        out_shape=[jax.ShapeDtypeStruct((n, w), dt) for w, dt in outs],
        compiler_params=_cparams(("parallel",)),
        name="in_proj",
    )(h2, norm_g, w_in_r, wa_pad, ba, qgain, kgain)


def _gla_levels(c):
    levels = []
    l = c // 2
    while l >= 1:
        levels.append(l)
        l //= 2
    return levels


def _gla_constants(c):
    t = np.arange(c)
    tri = (t[None, :] <= t[:, None]).astype(np.float32)
    mats, masks = [tri], []
    for l in _gla_levels(c):
        mid = (t // (2 * l)) * (2 * l) + l
        if l < SUBLANES:
            mats.append((t[None, :] <= (mid[:, None] - 1)).astype(np.float32))
        same = (t[:, None] // (2 * l)) == (t[None, :] // (2 * l))
        masks.append((same & ((t[:, None] % (2 * l)) >= l) & ((t[None, :] % (2 * l)) < l)).astype(np.float32))
    masks.append(np.eye(c, dtype=np.float32))
    return np.concatenate(mats, axis=0), np.stack(masks)


def _gla_kernel(q_ref, k_ref, v_ref, la_ref, s0_ref, gmat_ref, mask_ref, o_ref, s_out_ref, st_ref,
                *, chunk, front_pad):
    for bb in range(q_ref.shape[0]):
        _gla_chunk(bb, q_ref, k_ref, v_ref, la_ref, s0_ref, gmat_ref, mask_ref, o_ref, s_out_ref, st_ref,
                   chunk, front_pad)


def _gla_chunk(bb, q_ref, k_ref, v_ref, la_ref, s0_ref, gmat_ref, mask_ref, o_ref, s_out_ref, st_ref,
               chunk, front_pad):
    c = pl.program_id(1)
    levels = _gla_levels(chunk)
    w = GLA_QK_W

    @pl.when(c == 0)
    def _():
        st_ref[bb] = s0_ref[bb].T

    q = q_ref[bb]
    k = k_ref[bb]
    la = la_ref[bb]
    if front_pad:
        row = lax.broadcasted_iota(jnp.int32, la.shape, 0) + c * chunk
        la = jnp.where(row < front_pad, 0.0, la)
    la_hi, la_lo = _split_bf16(la)
    p = _dot(gmat_ref[...], jnp.concatenate([la_hi, la_lo], axis=1))
    p = p[:, :w] + p[:, w:]
    b = p[:chunk]
    lane = _lane_iota((1, w))
    head_masks = [(lane >= h * GLA_DK) & (lane < (h + 1) * GLA_DK) for h in range(GLA_HEADS)]

    scores = [jnp.zeros((chunk, chunk), F32) for _ in range(GLA_HEADS)]
    n_mat = 0
    for i, l in enumerate(levels):
        if l < SUBLANES:
            n_mat += 1
            r = p[n_mat * chunk:(n_mat + 1) * chunk]
        else:
            r = jnp.concatenate([jnp.broadcast_to(b[g + l - 1:g + l], (2 * l, w))
                                 for g in range(0, chunk, 2 * l)], axis=0)
        qt = q * jnp.exp(jnp.minimum(b - r, 0.0))
        kt = (k * jnp.exp(jnp.minimum(r - b, 0.0))).astype(BF16)
        for h in range(GLA_HEADS):
            qh = jnp.where(head_masks[h], qt, 0.0).astype(BF16)
            scores[h] = scores[h] + mask_ref[i] * _dot_nt(qh, kt)
    kb = k.astype(BF16)
    for h in range(GLA_HEADS):
        qh = jnp.where(head_masks[h], q, 0.0).astype(BF16)
        scores[h] = scores[h] + mask_ref[len(levels)] * _dot_nt(qh, kb)

    st = st_ref[bb]
    st_b = st.astype(BF16)
    b_last = b[chunk - 1:chunk]
    q_in = q * jnp.exp(b)
    k_out = k * jnp.exp(b_last - b)
    upd = jnp.zeros_like(st)
    for h in range(GLA_HEADS):
        vh = v_ref[bb, :, h * GLA_DV:(h + 1) * GLA_DV]
        vhb = vh.astype(BF16)
        qh = jnp.where(head_masks[h], q_in, 0.0).astype(BF16)
        o_ref[bb, :, h * GLA_DV:(h + 1) * GLA_DV] = _dot(scores[h].astype(BF16), vhb) + _dot_nt(qh, st_b)
        kh = jnp.where(head_masks[h], k_out, 0.0).astype(BF16)
        upd = upd + _dot(vh.T.astype(BF16), kh)
    st_new = st * jnp.exp(b_last) + upd
    st_ref[bb] = st_new

    @pl.when(c == pl.num_programs(1) - 1)
    def _():
        s_out_ref[bb] = st_new.T


def _gla(q, k, v, la, s0, front_pad):
    bsz, t, _ = q.shape
    chunk = GLA_CHUNK
    gmat, masks = _gla_constants(chunk)
    gmat = jnp.asarray(gmat, BF16)
    masks = jnp.asarray(masks, F32)
    nb = max(d for d in range(1, GLA_BATCH + 1) if bsz % d == 0)
    tok = lambda w: pl.BlockSpec((nb, chunk, w), lambda b, c: (b, c, 0))
    per_b = pl.BlockSpec((nb, GLA_QK_W, GLA_DV), lambda b, c: (b, 0, 0))
    return pl.pallas_call(
        functools.partial(_gla_kernel, chunk=chunk, front_pad=front_pad),
        grid=(bsz // nb, t // chunk),
        in_specs=[tok(GLA_QK_W), tok(GLA_QK_W), tok(GLA_V_W), tok(GLA_QK_W), per_b,
                  pl.BlockSpec(gmat.shape, lambda b, c: (0, 0)),
                  pl.BlockSpec(masks.shape, lambda b, c: (0, 0, 0))],
        out_specs=[tok(GLA_V_W), per_b],
        out_shape=[jax.ShapeDtypeStruct((bsz, t, GLA_V_W), F32),
                   jax.ShapeDtypeStruct((bsz, GLA_QK_W, GLA_DV), F32)],
        scratch_shapes=[pltpu.VMEM((nb, GLA_DV, GLA_QK_W), F32)],
        compiler_params=_cparams(("parallel", "arbitrary")),
        name="gla",
    )(q, k, v, la, s0, gmat, masks)


def _softplus2(z):
    neg_abs = lax.bitcast_convert_type(lax.bitcast_convert_type(z, jnp.uint32) | jnp.uint32(0x80000000), F32)
    return jnp.maximum(z, 0.0) + jnp.log2(1.0 + jnp.exp2(neg_abs))


def _sb_prompt_kernel(bias_ref, q_ref, k_ref, v_ref, ntri_ref, o_ref, acc_ref, c0_ref, c1_ref, *, tq, tk):
    hp = pl.program_id(1)
    i = pl.program_id(2)
    ndiag = tq // tk
    q = q_ref[0].astype(F32)
    lane_q = _lane_iota(q.shape)
    q0 = jnp.where(lane_q < SB_DIM, q, jnp.where(lane_q < SB_DIM + 2, 1.0, 0.0)).astype(BF16)
    q1 = jnp.where(lane_q >= SB_DIM, q, jnp.where(lane_q < 2, 1.0, 0.0)).astype(BF16)
    ntri = ntri_ref[...]
    lane_k = _lane_iota((tk, LANES))
    low_k = lane_k < SB_DIM

    def bias_lanes(h, first_lane):
        hi = jnp.full((tk, LANES), bias_ref[4 * hp + 2 * h], F32)
        lo = jnp.full((tk, LANES), bias_ref[4 * hp + 2 * h + 1], F32)
        return jnp.where(lane_k == first_lane, hi, jnp.where(lane_k == first_lane + 1, lo, 0.0)).astype(BF16)

    kbias0 = bias_lanes(0, SB_DIM)
    kbias1 = bias_lanes(1, 0)

    def head(qh, kh, vh, carry, mask):
        z = _dot_nt(qh, kh)
        sp = _softplus2(z)
        if mask is not None:
            sp = jnp.where(mask, sp, 0.0)
        between = _dot(sp.astype(BF16), ntri)
        a = jnp.exp2((z - sp) + between + jnp.concatenate([carry] * (tk // LANES), axis=1))
        if mask is not None:
            a = jnp.where(mask, a, 0.0)
        rowsum = jnp.broadcast_to(jnp.sum(sp, axis=-1, keepdims=True), carry.shape)
        return _dot(a.astype(BF16), vh), carry - rowsum

    def tile(kt, r0, mask, c0, c1):
        rows = pl.ds(pl.multiple_of(kt * tk, tk), tk)
        kb = k_ref[0, rows, :]
        vb = v_ref[0, rows, :]
        zk = jnp.zeros_like(vb)
        o0, c0 = head(q0[r0:], jnp.where(low_k, kb, kbias0), jnp.where(low_k, vb, zk), c0, mask)
        o1, c1 = head(q1[r0:], jnp.where(low_k, kbias1, kb), jnp.where(low_k, zk, vb), c1, mask)
        return o0 + o1, c0, c1

    nfull = i * ndiag

    acc = jnp.zeros((tq, LANES), F32)
    c0 = jnp.zeros((tq, LANES), F32)
    c1 = jnp.zeros((tq, LANES), F32)
    for d in reversed(range(ndiag)):
        r0 = d * tk
        nr = tq - r0
        mask = lax.broadcasted_iota(jnp.int32, (nr, tk), 0) > lax.broadcasted_iota(jnp.int32, (nr, tk), 1)
        o, c0n, c1n = tile(nfull + d, r0, mask, c0[r0:], c1[r0:])
        tail_rows = lambda full, new: new if r0 == 0 else jnp.concatenate([full[:r0], new], axis=0)
        acc = tail_rows(acc, acc[r0:] + o)
        c0 = tail_rows(c0, c0n)
        c1 = tail_rows(c1, c1n)
    acc_ref[...] = acc
    c0_ref[...] = c0
    c1_ref[...] = c1

    def trip(first_kt, ntiles):
        c0 = c0_ref[...]
        c1 = c1_ref[...]
        total = None
        for u in range(ntiles):
            o, c0, c1 = tile(first_kt - u, 0, None, c0, c1)
            total = o if total is None else total + o
        acc_ref[...] += total
        c0_ref[...] = c0
        c1_ref[...] = c1

    per_trip = SB_TRIP_BLOCKS * ndiag

    def body(it, carry):
        trip(nfull - 1 - it * per_trip, per_trip)
        return carry

    lax.fori_loop(0, i // SB_TRIP_BLOCKS, body, 0)
    for rem in range(1, SB_TRIP_BLOCKS):
        @pl.when(i % SB_TRIP_BLOCKS == rem)
        def _():
            trip(rem * ndiag - 1, rem * ndiag)
    o_ref[0] = acc_ref[...]


def _sb_prompt(q, k, v, bias):
    bsz, t, _ = q.shape
    tq, tk = SB_TQ, SB_TK
    t_idx = np.arange(tk)
    ntri = jnp.asarray(-(t_idx[:, None] > t_idx[None, :]).astype(np.float32), BF16)
    grid_spec = pltpu.PrefetchScalarGridSpec(
        num_scalar_prefetch=1,
        grid=(bsz, SB_HEADS // 2, t // tq),
        in_specs=[pl.BlockSpec((1, tq, LANES), lambda b, hp, i, bias: (b, i, hp)),
                  pl.BlockSpec((1, t, LANES), lambda b, hp, i, bias: (b, 0, hp)),
                  pl.BlockSpec((1, t, LANES), lambda b, hp, i, bias: (b, 0, hp)),
                  pl.BlockSpec((tk, tk), lambda b, hp, i, bias: (0, 0))],
        out_specs=pl.BlockSpec((1, tq, LANES), lambda b, hp, i, bias: (b, i, hp)),
        scratch_shapes=[pltpu.VMEM((tq, LANES), F32)] * 3,
    )
    return pl.pallas_call(
        functools.partial(_sb_prompt_kernel, tq=tq, tk=tk),
        grid_spec=grid_spec,
        out_shape=jax.ShapeDtypeStruct((bsz, t, SB_W), F32),
        compiler_params=_cparams(("parallel", "parallel", "arbitrary")),
        name="sb_prompt",
    )(bias, q, k, v, ntri)


def _sb_sample_kernel(pt_ref, q_ref, bias_ref, kn_ref, vn_ref, *rest, n_new, page, pages_per_step):
    kc_refs = rest[:pages_per_step]
    vc_refs = rest[pages_per_step:2 * pages_per_step]
    ntri_ref, o_ref, acc_ref, car_ref = rest[2 * pages_per_step:]
    j = pl.program_id(1)
    rows = SB_HEADS * SUBLANES
    q = q_ref[0]
    bias = bias_ref[...]
    ntri = ntri_ref[...]

    def run(k_refs, v_refs, mask):
        n = len(k_refs)
        kt = jnp.concatenate([r[0].reshape(SB_W, page).astype(BF16) for r in k_refs], axis=1)
        vt = jnp.concatenate([r[0].reshape(SB_W, page).astype(BF16) for r in v_refs], axis=1)
        z_all = _dot(q, kt)
        zs, sps = [], []
        for r in range(n):
            z = z_all[:, r * page:(r + 1) * page] + bias
            sp = _softplus2(z)
            if mask is not None:
                sp = jnp.where(mask, sp, 0.0)
            zs.append(z)
            sps.append(sp)
        between = _dot(jnp.concatenate(sps, axis=0).astype(BF16), ntri)
        car = car_ref[...]
        a_list = []
        for r in range(n):
            a = jnp.exp2((zs[r] - sps[r]) + between[r * rows:(r + 1) * rows] + car)
            if mask is not None:
                a = jnp.where(mask, a, 0.0)
            a_list.append(a.astype(BF16))
            car = car - jnp.broadcast_to(jnp.sum(sps[r], axis=-1, keepdims=True), car.shape)
        acc_ref[...] += _dot_nt(jnp.concatenate(a_list, axis=1), vt)
        car_ref[...] = car

    @pl.when(j == 0)
    def _():
        acc_ref[...] = jnp.zeros_like(acc_ref)
        car_ref[...] = jnp.zeros_like(car_ref)
        t_i = lax.broadcasted_iota(jnp.int32, (rows, page), 0) % SUBLANES
        s_i = lax.broadcasted_iota(jnp.int32, (rows, page), 1)
        run([kn_ref], [vn_ref], (s_i < t_i) & (s_i < n_new))

    run(kc_refs, vc_refs, None)

    @pl.when(j == pl.num_programs(1) - 1)
    def _():
        acc = acc_ref[...]
        for h in range(SB_HEADS):
            o_ref[0, h] = acc[h * SUBLANES:(h + 1) * SUBLANES, h * SB_DIM:(h + 1) * SB_DIM]


def _sb_sample(q_bd, bias_rows, kt_new, vt_new, cache_kt, cache_vt, page_table, n_new):
    bsz = q_bd.shape[0]
    page = cache_kt.shape[3]
    n_pages = page_table.shape[1]
    pps = max(p for p in range(1, SB_SAMPLE_PAGES + 1) if n_pages % p == 0)
    rows = SB_HEADS * SUBLANES
    t_idx = np.arange(page)
    ntri = jnp.asarray(-(t_idx[:, None] > t_idx[None, :]).astype(np.float32), BF16)

    def cache_map(r):
        return lambda b, j, pt: (pt[b * n_pages + (n_pages - 1 - (j * pps + r))], 0, 0, 0)

    per_b3 = lambda b, j, pt: (b, 0, 0)
    per_b4 = lambda b, j, pt: (b, 0, 0, 0)
    page_block = (1, SB_HEADS, SB_DIM, page)
    grid_spec = pltpu.PrefetchScalarGridSpec(
        num_scalar_prefetch=1,
        grid=(bsz, n_pages // pps),
        in_specs=[pl.BlockSpec((1, rows, SB_W), per_b3),
                  pl.BlockSpec((rows, page), lambda b, j, pt: (0, 0)),
                  pl.BlockSpec(page_block, per_b4),
                  pl.BlockSpec(page_block, per_b4)]
                 + [pl.BlockSpec(page_block, cache_map(r)) for r in list(range(pps)) * 2]
                 + [pl.BlockSpec((page, page), lambda b, j, pt: (0, 0))],
        out_specs=pl.BlockSpec((1, SB_HEADS, SUBLANES, SB_DIM), per_b4),
        scratch_shapes=[pltpu.VMEM((rows, SB_W), F32), pltpu.VMEM((rows, page), F32)],
    )
    return pl.pallas_call(
        functools.partial(_sb_sample_kernel, n_new=n_new, page=page, pages_per_step=pps),
        grid_spec=grid_spec,
        out_shape=jax.ShapeDtypeStruct((bsz, SB_HEADS, SUBLANES, SB_DIM), F32),
        compiler_params=_cparams(("parallel", "arbitrary")),
        name="sb_sample",
    )(page_table.reshape(-1), q_bd, bias_rows, kt_new, vt_new, *([cache_kt] * pps), *([cache_vt] * pps), ntri)


def _mix_out_kernel(h_ref, og_ref, gr_ref, os_ref, ggain_ref, sgain_ref, wo_ref, fgain_ref, rw_ref, rb_ref,
                    h1_ref, xn_ref, eidx_ref, gate_ref):
    og = og_ref[...]
    parts = []
    for hh in range(GLA_HEADS):
        x = og[:, hh * GLA_DV:(hh + 1) * GLA_DV]
        parts.append(x * lax.rsqrt(jnp.mean(x * x, axis=-1, keepdims=True) + EPS))
    gr = gr_ref[...]
    og_n = (jnp.concatenate(parts, axis=1) * ggain_ref[...]) * (gr * (1.0 / (1.0 + jnp.exp(-gr))))
    os_n = _half_lane_rms(os_ref[...], sgain_ref[...])
    h1 = h_ref[...] + (_dot(og_n.astype(BF16), wo_ref[:GLA_V_W, :]) + _dot(os_n.astype(BF16), wo_ref[GLA_V_W:, :]))
    h1_ref[...] = h1

    xn = (h1 * lax.rsqrt(jnp.mean(h1 * h1, axis=-1, keepdims=True) + EPS)) * fgain_ref[...]
    for j in range(xn.shape[1] // LANES):
        xn_ref[pl.ds(j, xn.shape[0], stride=SLAB), :] = xn[:, j * LANES:(j + 1) * LANES]

    x_hi, x_lo = _split_bf16(xn)
    rw = rw_ref[...]
    l2 = _dot(x_hi, rw) + _dot(x_lo, rw)
    logits = l2[:, :LANES] + l2[:, LANES:] + rb_ref[...]
    lane = _lane_iota(logits.shape).astype(F32)
    big = jnp.float32(4 * LANES)
    neg = jnp.float32(-jnp.inf)

    is_g = (lane >= N_EXPERTS) & (lane < N_EXPERTS + N_GROUPS)
    lg = jnp.where(is_g, logits, neg)
    mg = jnp.max(lg, axis=-1, keepdims=True)
    g_val = 1.0 / jnp.sum(jnp.exp(lg - mg), axis=-1, keepdims=True)
    g_idx = jnp.min(jnp.where(lg == mg, lane, big), axis=-1, keepdims=True) - N_EXPERTS

    in_g = (lane >= g_idx * EXPERTS_PER_GROUP) & (lane < (g_idx + 1) * EXPERTS_PER_GROUP)
    le = jnp.where(in_g, logits, neg)
    m1 = jnp.max(le, axis=-1, keepdims=True)
    se = jnp.sum(jnp.exp(le - m1), axis=-1, keepdims=True)
    i1 = jnp.min(jnp.where(le == m1, lane, big), axis=-1, keepdims=True)
    le2 = jnp.where(lane == i1, neg, le)
    m2 = jnp.max(le2, axis=-1, keepdims=True)
    i2 = jnp.min(jnp.where(le2 == m2, lane, big), axis=-1, keepdims=True)
    p1 = 1.0 / se
    p2 = jnp.exp(m2 - m1) / se
    tot = p1 + p2
    w1 = g_val * p1 / tot
    w2 = g_val * p2 / tot
    eidx_ref[...] = jnp.where(lane == 0, i1, jnp.where(lane == 1, i2, 0.0))[:, :SUBLANES].astype(jnp.int32)
    gate_ref[...] = jnp.where(lane == 0, w1, jnp.where(lane == 1, w2, 0.0))[:, :SUBLANES]


def _mix_out(h2, og, gr, osb, ggain, sgain, w_out_b, fgain, rw, rb):
    n, d = h2.shape
    tm = min(ROW_TILE, n)
    row = lambda w: pl.BlockSpec((tm, w), lambda i: (i, 0))
    full = lambda a: pl.BlockSpec(a.shape, lambda i: (0,) * a.ndim)
    return pl.pallas_call(
        _mix_out_kernel,
        grid=(n // tm,),
        in_specs=[row(d), row(GLA_V_W), row(GLA_V_W), row(SB_W), full(ggain), full(sgain), full(w_out_b),
                  full(fgain), full(rw), full(rb)],
        out_specs=[row(d), pl.BlockSpec((tm * SLAB, LANES), lambda i: (i, 0)), row(SUBLANES),
                   row(SUBLANES)],
        out_shape=[jax.ShapeDtypeStruct((n, d), F32), jax.ShapeDtypeStruct((n * SLAB, LANES), F32),
                   jax.ShapeDtypeStruct((n, SUBLANES), jnp.int32), jax.ShapeDtypeStruct((n, SUBLANES), F32)],
        compiler_params=_cparams(("parallel",)),
        name="mix_out",
    )(h2, og, gr, osb, ggain, sgain, w_out_b, fgain, rw, rb)


def _moe_slots_kernel(eidx_ref, ltri_ref, dest_ref, cnt_ref, counts_ref, run_ref, pstart_ref):
    ph = pl.program_id(0)
    i = pl.program_id(1)
    e = eidx_ref[...]
    tm = e.shape[0]
    lane = _lane_iota((tm, LANES))
    oh0 = (lane == e[:, 0:1]).astype(F32)
    oh1 = (lane == e[:, 1:2]).astype(F32)
    tot0 = jnp.sum(oh0, axis=0, keepdims=True)
    tot1 = jnp.sum(oh1, axis=0, keepdims=True)

    @pl.when((ph == 0) & (i == 0))
    def _():
        counts_ref[...] = jnp.zeros_like(counts_ref)

    @pl.when(ph == 0)
    def _():
        counts_ref[...] += tot0 + tot1

    @pl.when((ph == 1) & (i == 0))
    def _():
        cnt = counts_ref[...]
        padded = jnp.floor((cnt + (MOE_BLOCK - 1)) * (1.0 / MOE_BLOCK)) * MOE_BLOCK
        x = jnp.broadcast_to(padded, (SUBLANES, LANES))
        l8 = _lane_iota((SUBLANES, LANES))
        s = 1
        while s < LANES:
            x = x + jnp.where(l8 >= s, pltpu.roll(x, s, axis=1), 0.0)
            s *= 2
        pstart_ref[...] = x[0:1] - padded
        run_ref[...] = jnp.zeros_like(run_ref)
        cnt_ref[...] = jnp.broadcast_to(cnt, (SUBLANES, LANES)).astype(jnp.int32)

    @pl.when(ph == 1)
    def _():
        ltri = ltri_ref[...]
        base0 = run_ref[...] + pstart_ref[...]
        c0 = _dot(ltri, oh0.astype(BF16))
        c1 = _dot(ltri, oh1.astype(BF16))
        d0 = jnp.sum(oh0 * (base0 + c0), axis=-1, keepdims=True)
        d1 = jnp.sum(oh1 * (base0 + tot0 + c1), axis=-1, keepdims=True)
        l8 = _lane_iota((tm, LANES))
        dest_ref[...] = jnp.where(l8 == 0, d0, jnp.where(l8 == 1, d1, 0.0))[:, :SUBLANES].astype(jnp.int32)
        run_ref[...] += tot0 + tot1


def _moe_slots(eidx):
    n = eidx.shape[0]
    tm = MOE_TILE
    t_idx = np.arange(tm)
    ltri = jnp.asarray((t_idx[None, :] < t_idx[:, None]).astype(np.float32), BF16)
    return pl.pallas_call(
        _moe_slots_kernel,
        grid=(2, n // tm),
        in_specs=[pl.BlockSpec((tm, SUBLANES), lambda ph, i: (i, 0)),
                  pl.BlockSpec((tm, tm), lambda ph, i: (0, 0))],
        out_specs=[pl.BlockSpec((tm, SUBLANES), lambda ph, i: (i * ph, 0)),
                   pl.BlockSpec((SUBLANES, LANES), lambda ph, i: (0, 0))],
        out_shape=[jax.ShapeDtypeStruct((n, SUBLANES), jnp.int32),
                   jax.ShapeDtypeStruct((SUBLANES, LANES), jnp.int32)],
        scratch_shapes=[pltpu.VMEM((1, LANES), F32)] * 3,
        compiler_params=_cparams(("arbitrary", "arbitrary")),
        name="moe_slots",
    )(eidx, ltri)


def _load_slots(d0_hbm, d1_hbm, d0_s, d1_s, isem, tile, tm):
    base = pl.multiple_of(tile * tm, tm)
    c0 = pltpu.make_async_copy(d0_hbm.at[pl.ds(base, tm)], d0_s, isem.at[0])
    c1 = pltpu.make_async_copy(d1_hbm.at[pl.ds(base, tm)], d1_s, isem.at[1])
    c0.start()
    c1.start()
    c0.wait()
    c1.wait()


def _on_parity(i, fn):
    for s in (0, 1):
        pl.when(i % 2 == s)(functools.partial(fn, s))


def _moe_dispatch_kernel(cnt_ref, end_ref, d0_hbm, d1_hbm, x_ref, xs_hbm, d0_s, d1_s, zero_ref, isem, zsem, sem,
                         *, tm):
    i = pl.program_id(0)

    @pl.when(i == 0)
    def _():
        zero_ref[...] = jnp.zeros_like(zero_ref)

        def clear(e):
            first = pl.multiple_of((end_ref[e] - MOE_BLOCK) * SLAB, SLAB)
            return pltpu.make_async_copy(zero_ref, xs_hbm.at[pl.ds(first, MOE_BLOCK * SLAB)], zsem)

        for e in range(N_EXPERTS):
            pl.when(cnt_ref[e] > 0)(lambda e=e: clear(e).start())
        for e in range(N_EXPERTS):
            pl.when(cnt_ref[e] > 0)(lambda e=e: clear(e).wait())

        used = end_ref[N_EXPERTS - 1] // MOE_BLOCK
        total = xs_hbm.shape[0] // (MOE_BLOCK * SLAB)

        def spare(b):
            first = pl.multiple_of(b * (MOE_BLOCK * SLAB), SLAB)
            return pltpu.make_async_copy(zero_ref, xs_hbm.at[pl.ds(first, MOE_BLOCK * SLAB)], zsem)

        lax.fori_loop(used, total, lambda b, c: (spare(b).start(), c)[1], 0)
        lax.fori_loop(used, total, lambda b, c: (spare(b).wait(), c)[1], 0)

    _load_slots(d0_hbm, d1_hbm, d0_s, d1_s, isem, i, tm)

    def copies(r):
        src = x_ref.at[_slab(r)]
        return (pltpu.make_async_copy(src, xs_hbm.at[_slab(d0_s[r])], sem.at[0]),
                pltpu.make_async_copy(src, xs_hbm.at[_slab(d1_s[r])], sem.at[1]))

    def issue(r, carry):
        a, b = copies(r)
        a.start()
        b.start()
        return carry

    def drain(r, carry):
        a, b = copies(r)
        a.wait()
        b.wait()
        return carry

    lax.fori_loop(0, tm, issue, 0, unroll=DMA_LOOP_UNROLL)
    lax.fori_loop(0, tm, drain, 0, unroll=DMA_LOOP_UNROLL)


def _moe_dispatch(counts, end_rows, d0, d1, xn_slab, m_pad):
    n = xn_slab.shape[0] // SLAB
    tm = MOE_TILE
    grid_spec = pltpu.PrefetchScalarGridSpec(
        num_scalar_prefetch=2,
        grid=(n // tm,),
        in_specs=[pl.BlockSpec(memory_space=pl.ANY), pl.BlockSpec(memory_space=pl.ANY),
                  pl.BlockSpec((tm * SLAB, LANES), lambda i, cnt, end: (i, 0))],
        out_specs=pl.BlockSpec(memory_space=pl.ANY),
        scratch_shapes=[pltpu.SMEM((tm,), jnp.int32), pltpu.SMEM((tm,), jnp.int32),
                        pltpu.VMEM((MOE_BLOCK * SLAB, LANES), F32),
                        pltpu.SemaphoreType.DMA((2,)), pltpu.SemaphoreType.DMA(()),
                        pltpu.SemaphoreType.DMA((2,))],
    )
    return pl.pallas_call(
        functools.partial(_moe_dispatch_kernel, tm=tm),
        grid_spec=grid_spec,
        out_shape=jax.ShapeDtypeStruct((m_pad * SLAB, LANES), F32),
        compiler_params=_cparams(("arbitrary",)),
        name="moe_dispatch",
    )(counts, end_rows, d0, d1, xn_slab)


def _moe_experts_kernel(be_ref, nu_ref, xs_ref, wg_ref, wu_ref, wd_ref, y_ref, wgb_ref, wub_ref, wdb_ref):
    i = pl.program_id(0)
    rows = xs_ref.shape[0] // SLAB

    @pl.when((i == 0) | (be_ref[i] != be_ref[jnp.maximum(i - 1, 0)]))
    def _():
        wgb_ref[...] = wg_ref[0].astype(BF16)
        wub_ref[...] = wu_ref[0].astype(BF16)
        wdb_ref[...] = wd_ref[0].astype(BF16)

    @pl.when(i < nu_ref[0])
    def _():
        x = jnp.concatenate([xs_ref[pl.ds(j, rows, stride=SLAB), :] for j in range(SLAB)], axis=1).astype(BF16)
        g = _dot(x, wgb_ref[...])
        u = _dot(x, wub_ref[...])
        hdn = (g * (1.0 / (1.0 + jnp.exp(-g)))) * u
        y = _dot(hdn.astype(BF16), wdb_ref[...])
        for j in range(SLAB):
            y_ref[pl.ds(j, rows, stride=SLAB), :] = y[:, j * LANES:(j + 1) * LANES]

    @pl.when(i >= nu_ref[0])
    def _():
        y_ref[...] = jnp.zeros_like(y_ref)


def _moe_experts(block_e, n_used, xs, wg, wu, wd):
    m_pad = xs.shape[0] // SLAB
    nb = m_pad // MOE_BLOCK
    d, de = wg.shape[1], wg.shape[2]
    rows = pl.BlockSpec((MOE_BLOCK * SLAB, LANES), lambda i, be, nu: (i, 0))
    grid_spec = pltpu.PrefetchScalarGridSpec(
        num_scalar_prefetch=2,
        grid=(nb,),
        in_specs=[pl.BlockSpec((MOE_BLOCK * SLAB, LANES), lambda i, be, nu: (jnp.minimum(i, nu[0] - 1), 0)),
                  pl.BlockSpec((1, d, de), lambda i, be, nu: (be[i], 0, 0)),
                  pl.BlockSpec((1, d, de), lambda i, be, nu: (be[i], 0, 0)),
                  pl.BlockSpec((1, de, d), lambda i, be, nu: (be[i], 0, 0))],
        out_specs=rows,
        scratch_shapes=[pltpu.VMEM((d, de), BF16), pltpu.VMEM((d, de), BF16), pltpu.VMEM((de, d), BF16)],
    )
    return pl.pallas_call(
        _moe_experts_kernel,
        grid_spec=grid_spec,
        out_shape=jax.ShapeDtypeStruct((m_pad * SLAB, LANES), F32),
        compiler_params=_cparams(("arbitrary",)),
        name="moe_experts",
    )(block_e, n_used, xs, wg, wu, wd)


def _moe_combine_kernel(d0_hbm, d1_hbm, gate_ref, h1_ref, y_hbm, out_ref, d0_a, d1_a, d0_b, d1_b,
                        buf0_a, buf1_a, buf0_b, buf1_b, isem, sem, *, tm):
    i = pl.program_id(0)
    last = pl.num_programs(0) - 1
    idx = ((d0_a, d1_a), (d0_b, d1_b))
    bufs = ((buf0_a, buf1_a), (buf0_b, buf1_b))
    cr = MOE_COMBINE_ROWS

    def copies(s, r):
        return (pltpu.make_async_copy(y_hbm.at[_slab(idx[s][0][r])], bufs[s][0].at[_slab(r)], sem.at[s, 0]),
                pltpu.make_async_copy(y_hbm.at[_slab(idx[s][1][r])], bufs[s][1].at[_slab(r)], sem.at[s, 1]))

    def gather(tile, s):
        _load_slots(d0_hbm, d1_hbm, idx[s][0], idx[s][1], isem, tile, tm)

        def body(r, carry):
            a, b = copies(s, r)
            a.start()
            b.start()
            return carry
        lax.fori_loop(0, tm, body, 0, unroll=DMA_LOOP_UNROLL)

    def run(s):
        @pl.when(i == 0)
        def _():
            gather(i, s)

        @pl.when(i < last)
        def _():
            gather(i + 1, 1 - s)

        def drain(r, carry):
            a, b = copies(s, r)
            a.wait()
            b.wait()
            return carry
        lax.fori_loop(0, tm, drain, 0, unroll=DMA_LOOP_UNROLL)

        def chunk(c, carry):
            r0 = pl.multiple_of(c * cr, cr)
            rows = pl.ds(r0, cr)
            w0 = jnp.broadcast_to(gate_ref[rows, 0:1], (cr, LANES))
            w1 = jnp.broadcast_to(gate_ref[rows, 1:2], (cr, LANES))
            for j in range(SLAB):
                srows = pl.ds(r0 * SLAB + j, cr, stride=SLAB)
                y = bufs[s][0][srows, :] * w0 + bufs[s][1][srows, :] * w1
                out_ref[rows, j * LANES:(j + 1) * LANES] = h1_ref[rows, j * LANES:(j + 1) * LANES] + y
            return carry
        lax.fori_loop(0, tm // cr, chunk, 0)

    _on_parity(i, run)


def _moe_combine(d0, d1, gate, h1, y_rows):
    n, d = h1.shape
    tm = MOE_TILE
    return pl.pallas_call(
        functools.partial(_moe_combine_kernel, tm=tm),
        grid=(n // tm,),
        in_specs=[pl.BlockSpec(memory_space=pl.ANY), pl.BlockSpec(memory_space=pl.ANY),
                  pl.BlockSpec((tm, SUBLANES), lambda i: (i, 0)),
                  pl.BlockSpec((tm, d), lambda i: (i, 0)),
                  pl.BlockSpec(memory_space=pl.ANY)],
        out_specs=pl.BlockSpec((tm, d), lambda i: (i, 0)),
        out_shape=jax.ShapeDtypeStruct((n, d), F32),
        scratch_shapes=[pltpu.SMEM((tm,), jnp.int32)] * 4 + [pltpu.VMEM((tm * SLAB, LANES), F32)] * 4
                       + [pltpu.SemaphoreType.DMA((2,)), pltpu.SemaphoreType.DMA((2, 2))],
        compiler_params=_cparams(("arbitrary",)),
        name="moe_combine",
    )(d0, d1, gate, h1, y_rows)


def _moe(h1, xn_slab, eidx, gate, wg, wu, wd):
    n = h1.shape[0]
    assert h1.shape[1] == SLAB * LANES
    n_pad = -(-n // MOE_TILE) * MOE_TILE
    if n_pad != n:
        padr = lambda a, k: jnp.pad(a, ((0, k * (n_pad - n)),) + ((0, 0),) * (a.ndim - 1))
        h1, xn_slab, eidx, gate = padr(h1, 1), padr(xn_slab, SLAB), padr(eidx, 1), padr(gate, 1)
    dest, cnt = _moe_slots(eidx)
    d0, d1 = dest[:, 0], dest[:, 1]
    nb = (2 * n_pad) // MOE_BLOCK + N_EXPERTS
    counts = cnt[0, :N_EXPERTS]
    pend = jnp.cumsum((counts + MOE_BLOCK - 1) // MOE_BLOCK)
    block_e = jnp.minimum(jnp.sum(pend[None, :] <= jnp.arange(nb, dtype=jnp.int32)[:, None], axis=1),
                          N_EXPERTS - 1).astype(jnp.int32)
    n_used = pend[-1:].astype(jnp.int32)
    xs = _moe_dispatch(counts, (pend * MOE_BLOCK).astype(jnp.int32), d0, d1, xn_slab, nb * MOE_BLOCK)
    y_rows = _moe_experts(block_e, n_used, xs, wg, wu, wd)
    return _moe_combine(d0, d1, gate, h1, y_rows)[:n]


def kernel(x_prompt, x_sample, cache_sb_k, cache_sb_v, state_gla, page_table, meta_tokens, norm_mix_gain, w_in, gla_w_alpha, gla_b_alpha, gla_out_gain, sb_q_gain, sb_k_gain, sb_logit_bias, sb_out_gain, w_out, norm_ffn_gain, router_group, router_group_b, router_expert, router_expert_b, w_gate, w_up, w_down):
    bsz, seq, d = x_prompt.shape
    dbs, dseq, _ = x_sample.shape
    depth = w_in.shape[0]
    page = cache_sb_k.shape[2]
    t_real = seq + N_META
    t_pad = -(-t_real // SB_TQ) * SB_TQ
    fpad = t_pad - t_real

    hp = jnp.concatenate([jnp.zeros((bsz, fpad, d), x_prompt.dtype),
                          jnp.broadcast_to(meta_tokens[None].astype(x_prompt.dtype), (bsz, N_META, d)),
                          x_prompt], axis=1).reshape(bsz * t_pad, d)
    hs = x_sample.reshape(dbs * dseq, d)

    outs = {k: [] for k in ("kp", "vp", "sp", "ks", "vs", "ss")}
    for l in range(depth):
        w = w_in[l]
        w_in_r = jnp.concatenate([w[:, :1536], w[:, 1552:3088], w[:, 1536:1552],
                                  jnp.zeros((d, _W_IN_COLS - 3088), w.dtype)], axis=1).astype(BF16)
        wa_pad = jnp.pad(gla_w_alpha[l], ((0, LANES - GLA_RANK), (0, 0)))
        ba = gla_b_alpha[l][None]
        norm_g = norm_mix_gain[l][None]
        qgain = jnp.tile(sb_q_gain[l], SB_HEADS)[None]
        kgain = jnp.tile(sb_k_gain[l], SB_HEADS)[None]
        ggain = jnp.tile(gla_out_gain[l], GLA_HEADS)[None]
        sgain = jnp.tile(sb_out_gain[l], SB_HEADS)[None]
        fgain = norm_ffn_gain[l][None]
        w_out_b = w_out[l].astype(BF16)
        r_all = jnp.concatenate([router_expert[l].transpose(1, 0, 2).reshape(d, N_EXPERTS), router_group[l],
                                 jnp.zeros((d, LANES - N_EXPERTS - N_GROUPS), F32)], axis=1)
        r_hi = r_all.astype(BF16)
        r_lo = (r_all - r_hi.astype(F32)).astype(BF16)
        rw = jnp.concatenate([r_hi, r_lo], axis=1)
        rb = jnp.concatenate([router_expert_b[l].reshape(N_EXPERTS), router_group_b[l],
                              jnp.zeros((LANES - N_EXPERTS - N_GROUPS,), F32)])[None]
        wg, wu, wd = w_gate[l], w_up[l], w_down[l]
        bias2 = sb_logit_bias[l].astype(F32) * LOG2E
        b_hi = bias2.astype(BF16).astype(F32)
        b_lo = (bias2 - b_hi).astype(BF16).astype(F32)
        bias_hl = jnp.stack([b_hi, b_lo], axis=1).reshape(-1)

        gq, gk, gv, gr, la, qs, ks, vs, ksb, vsb = _in_proj(hp, norm_g, w_in_r, wa_pad, ba, qgain, kgain)
        b3 = lambda a: a.reshape(bsz, t_pad, a.shape[-1])
        o_g, s_p = _gla(b3(gq), b3(gk), b3(gv), b3(la), jnp.zeros((bsz, GLA_QK_W, GLA_DV), F32), fpad)
        o_s = _sb_prompt(b3(qs), b3(ksb), b3(vsb), bias_hl)
        h1, xn3, eidx, gate = _mix_out(hp, o_g.reshape(-1, GLA_V_W), gr, o_s.reshape(-1, SB_W), ggain, sgain,
                                       w_out_b, fgain, rw, rb)
        hp = _moe(h1, xn3, eidx, gate, wg, wu, wd)
        outs["kp"].append(b3(ks)[:, fpad:].reshape(bsz, t_real, SB_HEADS, SB_DIM))
        outs["vp"].append(b3(vs)[:, fpad:].reshape(bsz, t_real, SB_HEADS, SB_DIM))
        outs["sp"].append(s_p.reshape(bsz, GLA_HEADS, GLA_DK, GLA_DV))

        gq, gk, gv, gr, la, qs, ks, vs, ksb, vsb = _in_proj(hs, norm_g, w_in_r, wa_pad, ba, qgain, kgain)
        cpad = GLA_CHUNK - dseq
        c3 = lambda a: jnp.pad(a.reshape(dbs, dseq, a.shape[-1]), ((0, 0), (cpad, 0), (0, 0)))
        o_g, s_s = _gla(c3(gq), c3(gk), c3(gv), c3(la), state_gla[l].reshape(dbs, GLA_QK_W, GLA_DV), cpad)
        o_g = o_g[:, cpad:].reshape(dbs * dseq, GLA_V_W)
        q4 = qs.reshape(dbs, dseq, SB_HEADS, SB_DIM).transpose(0, 2, 1, 3)
        q4 = jnp.pad(q4, ((0, 0), (0, 0), (0, SUBLANES - dseq), (0, 0)))
        eye = jnp.eye(SB_HEADS, dtype=q4.dtype)
        q_bd = (q4[:, :, :, None, :] * eye[None, :, None, :, None]).reshape(dbs, SB_HEADS * SUBLANES, SB_W)
        bias_rows = jnp.broadcast_to(jnp.repeat(bias2, SUBLANES)[:, None], (SB_HEADS * SUBLANES, page))
        to_t = lambda a: jnp.pad(a.reshape(dbs, dseq, SB_HEADS, SB_DIM).transpose(0, 2, 3, 1),
                                 ((0, 0), (0, 0), (0, 0), (0, page - dseq)))
        cache_kt = cache_sb_k[l].transpose(0, 2, 3, 1)
        cache_vt = cache_sb_v[l].transpose(0, 2, 3, 1)
        o4 = _sb_sample(q_bd, bias_rows, to_t(ks), to_t(vs), cache_kt, cache_vt, page_table, dseq)
        o_s = o4[:, :, :dseq].transpose(0, 2, 1, 3).reshape(dbs * dseq, SB_W)
        h1, xn3, eidx, gate = _mix_out(hs, o_g, gr, o_s, ggain, sgain, w_out_b, fgain, rw, rb)
        hs = _moe(h1, xn3, eidx, gate, wg, wu, wd)
        outs["ks"].append(ks.reshape(dbs, dseq, SB_HEADS, SB_DIM))
        outs["vs"].append(vs.reshape(dbs, dseq, SB_HEADS, SB_DIM))
        outs["ss"].append(s_s.reshape(dbs, GLA_HEADS, GLA_DK, GLA_DV))

    y_prompt = hp.reshape(bsz, t_pad, d)[:, fpad + N_META:]
    y_sample = hs.reshape(dbs, dseq, d)
    return (y_prompt, y_sample, jnp.stack(outs["kp"]), jnp.stack(outs["vp"]), jnp.stack(outs["sp"]),
            jnp.stack(outs["ks"]), jnp.stack(outs["vs"]), jnp.stack(outs["ss"]))
```
